```python
import math
import jax, jax.numpy as jnp
from jax import lax
import numpy as np

D_MODEL = 1024
BATCH = 4
SEQ = 4096
DEPTH = 4

HEAD_DIM = 64
CHUNK = 128
RET_HEADS = 6
RET_WIDTH = RET_HEADS * HEAD_DIM
MLP_GROUPS = 4
MLP_WIDTH = MLP_GROUPS * HEAD_DIM
ATT_HEADS = 6
ATT_KV_HEADS = 2
ATT_WIDTH = ATT_HEADS * HEAD_DIM
ATT_KV_WIDTH = ATT_KV_HEADS * HEAD_DIM
WINDOW = 128
MIX_WIDTH = RET_WIDTH + MLP_WIDTH + ATT_WIDTH
IN_WIDTH = 4 * RET_WIDTH + 2 * MLP_WIDTH + ATT_WIDTH + 2 * ATT_KV_WIDTH
D_FF = -(-8 * D_MODEL // (3 * 256)) * 256
RET_DECAY_BASE = 5.0
NORM_EPS = 1e-6

kernel_name = "hybrid_retention_gmlp_swa_trunk"


def rms_norm(x, g):
    xf = x.astype(jnp.float32)
    y = xf * lax.rsqrt(jnp.mean(xf * xf, axis=-1, keepdims=True) + NORM_EPS)
    return (y * g.astype(jnp.float32)).astype(x.dtype)


def alibi_slopes(n):
    def pow2(m):
        start = 2.0 ** (-(2.0 ** -(math.log2(m) - 3)))
        return [start * start ** i for i in range(m)]
    if math.log2(n).is_integer():
        s = pow2(n)
    else:
        c = 2 ** int(math.floor(math.log2(n)))
        s = pow2(c) + pow2(2 * c)[0::2][: n - c]
    return jnp.asarray(np.array(s, dtype=np.float32))


def retention(q, k, v):
    B, T, H, d = q.shape
    N = T // CHUNK
    q = q.reshape(B, N, CHUNK, H, d)
    k = k.reshape(B, N, CHUNK, H, d) * (d ** -0.5)
    v = v.reshape(B, N, CHUNK, H, d)
    log_g = jnp.log1p(-(2.0 ** (-RET_DECAY_BASE - jnp.arange(H, dtype=jnp.float32))))
    pos = jnp.arange(CHUNK, dtype=jnp.float32)
    diff = pos[:, None] - pos[None, :]
    intra_decay = jnp.where(diff[None] >= 0,
                            jnp.exp(log_g[:, None, None] * jnp.maximum(diff, 0.0)[None]), 0.0)
    scores = jnp.einsum('bnihd,bnjhd->bnhij', q, k) * intra_decay
    intra = jnp.einsum('bnhij,bnjhe->bnihe', scores, v)
    tail = jnp.exp(log_g[None, :] * (CHUNK - 1.0 - pos)[:, None])
    kv = jnp.einsum('bnjhd,bnjhe->nbhde', k * tail[None, None, :, :, None], v)
    chunk_decay = jnp.exp(log_g * CHUNK)[None, :, None, None]

    def step(state, kv_n):
        return state * chunk_decay + kv_n, state

    _, s_prev = lax.scan(step, jnp.zeros_like(kv[0]), kv)
    head = jnp.exp(log_g[None, :] * (pos + 1.0)[:, None])
    cross = jnp.einsum('bnihd,nbhde->bnihe', q * head[None, None, :, :, None], s_prev)
    return (intra + cross).reshape(B, T, H, d)


def spatial_gating(z, ln_g, ln_b, w_s, b_s):
    B, T, _ = z.shape
    N = T // CHUNK
    u, v = jnp.split(z.astype(jnp.float32), 2, axis=-1)
    mu = jnp.mean(v, axis=-1, keepdims=True)
    var = jnp.mean(jnp.square(v - mu), axis=-1, keepdims=True)
    v = (v - mu) * lax.rsqrt(var + NORM_EPS) * ln_g.astype(jnp.float32) + ln_b.astype(jnp.float32)
    vb = v.reshape(B, N, CHUNK, MLP_GROUPS, HEAD_DIM)
    causal = jnp.tril(jnp.ones((CHUNK, CHUNK), jnp.float32))
    ws = w_s.astype(jnp.float32) * causal[None]
    mixed = jnp.einsum('gij,bnjgc->bnigc', ws, vb) + b_s.astype(jnp.float32).T[None, None, :, :, None]
    return u * mixed.reshape(B, T, MLP_WIDTH)


def sliding_window_attention(q, k, v, sinks):
    B, T, H, d = q.shape
    Hkv = k.shape[2]
    G = H // Hkv
    N = T // CHUNK
    qb = q.astype(jnp.float32).reshape(B, N, CHUNK, Hkv, G, d) * (d ** -0.5)

    def band(t):
        tb = t.astype(jnp.float32).reshape(B, N, CHUNK, Hkv, d)
        prev = jnp.pad(tb, ((0, 0), (1, 0), (0, 0), (0, 0), (0, 0)))[:, :-1]
        return jnp.concatenate([prev, tb], axis=2)

    kb, vb = band(k), band(v)
    s = jnp.einsum('bnqhgd,bnkhd->bhgnqk', qb, kb)
    qi = jnp.arange(CHUNK)
    kj = jnp.arange(2 * CHUNK)
    dist = CHUNK + qi[:, None] - kj[None, :]
    key_pos = jnp.arange(N)[:, None] * CHUNK - CHUNK + kj[None, :]
    allowed = ((dist >= 0) & (dist < WINDOW))[None] & (key_pos >= 0)[:, None, :]
    slopes = alibi_slopes(H).reshape(Hkv, G)
    s = s - slopes[:, :, None, None, None] * dist.astype(jnp.float32)
    s = jnp.where(allowed, s, -jnp.inf)
    sink = jnp.broadcast_to(sinks.astype(jnp.float32).reshape(1, Hkv, G, 1, 1, 1), s.shape[:-1] + (1,))
    p = jax.nn.softmax(jnp.concatenate([s, sink], axis=-1), axis=-1)[..., :-1]
    out = jnp.einsum('bhgnqk,bnkhd->bnqhgd', p, vb)
    return out.reshape(B, T, H, d)


def setup_inputs(seed: int = 0) -> dict:
    key = jax.random.key(seed)
    ks = jax.random.split(key, 16)
    f32 = jnp.float32

    def nrm(k, shape, scale):
        return jax.random.normal(k, shape, f32) * scale

    return {
        "x": nrm(ks[0], (BATCH, SEQ, D_MODEL), 1.0),
        "norm1_g": 1.0 + nrm(ks[1], (DEPTH, D_MODEL), 0.02),
        "w_in": nrm(ks[2], (DEPTH, D_MODEL, IN_WIDTH), D_MODEL ** -0.5),
        "ret_norm_g": 1.0 + nrm(ks[3], (DEPTH, RET_WIDTH), 0.02),
        "mlp_ln_g": 1.0 + nrm(ks[4], (DEPTH, MLP_WIDTH), 0.02),
        "mlp_ln_b": nrm(ks[5], (DEPTH, MLP_WIDTH), 0.02),
        "w_spatial": nrm(ks[6], (DEPTH, MLP_GROUPS, CHUNK, CHUNK), CHUNK ** -0.5),
        "b_spatial": 1.0 + nrm(ks[7], (DEPTH, MLP_GROUPS, CHUNK), 0.02),
        "attn_sinks": nrm(ks[8], (DEPTH, ATT_HEADS), 0.5),
        "w_out": nrm(ks[9], (DEPTH, MIX_WIDTH, D_MODEL), MIX_WIDTH ** -0.5),
        "norm2_g": 1.0 + nrm(ks[10], (DEPTH, D_MODEL), 0.02),
        "w_ffn_in": nrm(ks[11], (DEPTH, D_MODEL, 2 * D_FF), D_MODEL ** -0.5),
        "w_ffn_out": nrm(ks[12], (DEPTH, D_FF, D_MODEL), D_FF ** -0.5),
        "final_norm_g": 1.0 + nrm(ks[13], (D_MODEL,), 0.02),
    }


def reference(x, norm1_g, w_in, ret_norm_g, mlp_ln_g, mlp_ln_b, w_spatial, b_spatial,
              attn_sinks, w_out, norm2_g, w_ffn_in, w_ffn_out, final_norm_g):
    B, T, _ = x.shape
    sizes = [RET_WIDTH] * 4 + [2 * MLP_WIDTH] + [ATT_WIDTH, ATT_KV_WIDTH, ATT_KV_WIDTH]
    split_idx = np.cumsum(sizes)[:-1].tolist()
    for l in range(DEPTH):
        h = rms_norm(x, norm1_g[l])
        proj = h @ w_in[l]
        q_r, k_r, v_r, g_r, z_m, q_a, k_a, v_a = jnp.split(proj, split_idx, axis=-1)

        ret = retention(q_r.astype(jnp.float32).reshape(B, T, RET_HEADS, HEAD_DIM),
                        k_r.astype(jnp.float32).reshape(B, T, RET_HEADS, HEAD_DIM),
                        v_r.astype(jnp.float32).reshape(B, T, RET_HEADS, HEAD_DIM))
        ret = ret * lax.rsqrt(jnp.mean(ret * ret, axis=-1, keepdims=True) + NORM_EPS)
        ret = ret.reshape(B, T, RET_WIDTH) * ret_norm_g[l].astype(jnp.float32)
        ret_out = (jax.nn.silu(g_r.astype(jnp.float32)) * ret).astype(x.dtype)

        mlp_out = spatial_gating(jax.nn.gelu(z_m, approximate=False), mlp_ln_g[l], mlp_ln_b[l],
                                 w_spatial[l], b_spatial[l]).astype(x.dtype)

        att = sliding_window_attention(q_a.reshape(B, T, ATT_HEADS, HEAD_DIM),
                                       k_a.reshape(B, T, ATT_KV_HEADS, HEAD_DIM),
                                       v_a.reshape(B, T, ATT_KV_HEADS, HEAD_DIM),
                                       attn_sinks[l])
        att_out = att.reshape(B, T, ATT_WIDTH).astype(x.dtype)

        mixed = jnp.concatenate([ret_out, mlp_out, att_out], axis=-1)
        x = x + mixed @ w_out[l]

        h = rms_norm(x, norm2_g[l])
        a, b = jnp.split(h @ w_ffn_in[l], 2, axis=-1)
        x = x + (jax.nn.silu(a) * b) @ w_ffn_out[l]
    return rms_norm(x, final_norm_g)
```

```python
import functools
import math

import jax
import jax.numpy as jnp
import numpy as np
from jax import lax
from jax.experimental import pallas as pl
from jax.experimental.pallas import tpu as pltpu

D_MODEL = 1024
HEAD_DIM = 64
CHUNK = 128
RET_HEADS = 6
RET_WIDTH = RET_HEADS * HEAD_DIM
MLP_GROUPS = 4
MLP_WIDTH = MLP_GROUPS * HEAD_DIM
ATT_HEADS = 6
ATT_KV_HEADS = 2
ATT_WIDTH = ATT_HEADS * HEAD_DIM
ATT_KV_WIDTH = ATT_KV_HEADS * HEAD_DIM
WINDOW = 128
IN_WIDTH = 4 * RET_WIDTH + 2 * MLP_WIDTH + ATT_WIDTH + 2 * ATT_KV_WIDTH
D_FF = -(-8 * D_MODEL // (3 * 256)) * 256
RET_DECAY_BASE = 5.0
NORM_EPS = 1e-6

LANES = 128
PAIR = 2 * HEAD_DIM
assert PAIR == LANES and CHUNK == LANES and WINDOW == CHUNK

Q_R, K_R, V_R, G_R = 0, RET_WIDTH, 2 * RET_WIDTH, 3 * RET_WIDTH
Z_M = 4 * RET_WIDTH
Q_A = Z_M + 2 * MLP_WIDTH
K_A = Q_A + ATT_WIDTH
V_A = K_A + ATT_KV_WIDTH
MIX_RET, MIX_MLP, MIX_ATT = 0, RET_WIDTH, RET_WIDTH + MLP_WIDTH

RET_PAIRS = RET_HEADS // 2
MLP_PAIRS = MLP_GROUPS // 2
ATT_PAIRS = ATT_HEADS // 2
ATT_GROUP = ATT_HEADS // ATT_KV_HEADS
ATT_HEAD_ORDER = tuple(h for p in range(ATT_PAIRS) for h in (p, p + ATT_GROUP))

MIX_TILE = 512
FFN_TILE = 512
FF_BLOCK = 256
MIX_VMEM_BYTES = 52 * 1024 * 1024
FFN_VMEM_BYTES = 56 * 1024 * 1024

BF16 = jnp.bfloat16
F32 = jnp.float32


def _dot(a, b):
    return jnp.dot(a, b, preferred_element_type=F32)


def _dot_nt(a, b):
    return lax.dot_general(a, b, (((1,), (1,)), ((), ())), preferred_element_type=F32)


def _left_half(shape):
    return lax.broadcasted_iota(jnp.int32, shape, len(shape) - 1) < HEAD_DIM


def _block_diag(pair):
    left = _left_half(pair.shape)
    zero = jnp.zeros_like(pair)
    return jnp.concatenate([jnp.where(left, pair, zero), jnp.where(left, zero, pair)], axis=0)


def _rms_norm(x, g):
    ms = jnp.mean(x * x, axis=-1, keepdims=True)
    return x * lax.rsqrt(ms + NORM_EPS) * g


def _gelu(x):
    return 0.5 * x * (1.0 + lax.erf(x * np.float32(math.sqrt(0.5))))


def _retention_pair(p, rows, proj_ref, state_ref, tabs, ret_g):
    intra_tab, head_tab, tail_tab, sdec_tab = tabs
    cols = slice(p * PAIR, (p + 1) * PAIR)
    q = proj_ref[rows, pl.ds(Q_R + p * PAIR, PAIR)]
    k = proj_ref[rows, pl.ds(K_R + p * PAIR, PAIR)]
    v = proj_ref[rows, pl.ds(V_R + p * PAIR, PAIR)]
    g = proj_ref[rows, pl.ds(G_R + p * PAIR, PAIR)]
    vb = v.astype(BF16)
    scores = _dot_nt(q.astype(BF16), _block_diag(k.astype(BF16))) * intra_tab[p]
    intra = _dot(scores.astype(BF16), _block_diag(vb))
    state = state_ref[p]
    cross = _dot((q * head_tab[p]).astype(BF16), state.astype(BF16))
    kt = (k * tail_tab[p]).T.astype(BF16)
    kv = _dot(kt, vb)
    same_head = _left_half((PAIR, PAIR)) == (
        lax.broadcasted_iota(jnp.int32, (PAIR, PAIR), 0) < HEAD_DIM)
    state_ref[p] = state * sdec_tab[p] + jnp.where(same_head, kv, 0.0)

    out = intra + cross
    sq = out * out
    lh = _left_half(out.shape)
    ss_l = jnp.sum(jnp.where(lh, sq, 0.0), axis=-1, keepdims=True)
    ss_r = jnp.sum(jnp.where(lh, 0.0, sq), axis=-1, keepdims=True)
    ms = jnp.where(lh, ss_l, ss_r) * np.float32(1.0 / HEAD_DIM)
    ret = out * lax.rsqrt(ms + NORM_EPS) * ret_g[:, cols]
    return jax.nn.silu(g) * ret


def _mixer_kernel(x_ref, g1_ref, w_in_ref, w_out_ref, ret_g_ref, ln_g_ref, ln_b_ref, ws_ref,
                  bs_ref, sink_ref, intra_ref, head_ref, tail_ref, sdec_ref, bias_ref,
                  o_ref, proj_ref, mixed_ref, state_ref, kprev_ref, vprev_ref):
    t = pl.program_id(1)

    @pl.when(t == 0)
    def _():
        state_ref[...] = jnp.zeros_like(state_ref)
        kprev_ref[...] = jnp.zeros_like(kprev_ref)
        vprev_ref[...] = jnp.zeros_like(vprev_ref)

    x = x_ref[...]
    h = _rms_norm(x, g1_ref[...]).astype(BF16)
    proj_ref[...] = _dot(h, w_in_ref[...])

    ret_g = ret_g_ref[...]
    ln_g = ln_g_ref[...]
    ln_b = ln_b_ref[...]
    causal = lax.broadcasted_iota(jnp.int32, (CHUNK, CHUNK), 0) >= lax.broadcasted_iota(
        jnp.int32, (CHUNK, CHUNK), 1)
    tabs = (intra_ref, head_ref, tail_ref, sdec_ref)
    first_tile = (t == 0).astype(jnp.int32)

    for c in range(MIX_TILE // CHUNK):
        rows = pl.ds(c * CHUNK, CHUNK)

        for p in range(RET_PAIRS):
            ret = _retention_pair(p, rows, proj_ref, state_ref, tabs, ret_g)
            mixed_ref[rows, pl.ds(MIX_RET + p * PAIR, PAIR)] = ret.astype(BF16)

        z = _gelu(proj_ref[rows, pl.ds(Z_M, 2 * MLP_WIDTH)])
        u = z[:, :MLP_WIDTH]
        v = z[:, MLP_WIDTH:]
        mu = jnp.mean(v, axis=-1, keepdims=True)
        var = jnp.mean(jnp.square(v - mu), axis=-1, keepdims=True)
        vn = ((v - mu) * lax.rsqrt(var + NORM_EPS) * ln_g + ln_b).astype(BF16)
        for p in range(MLP_PAIRS):
            cols = slice(p * PAIR, (p + 1) * PAIR)
            ws = jnp.concatenate(
                [jnp.where(causal, ws_ref[2 * p], 0.0), jnp.where(causal, ws_ref[2 * p + 1], 0.0)],
                axis=1).astype(BF16)
            gate = _dot(ws, _block_diag(vn[:, cols])) + bs_ref[:, cols]
            mixed_ref[rows, pl.ds(MIX_MLP + p * PAIR, PAIR)] = (u[:, cols] * gate).astype(BF16)

        kcur = (proj_ref[rows, pl.ds(K_A, PAIR)] * np.float32(HEAD_DIM ** -0.5)).astype(BF16)
        vcur = proj_ref[rows, pl.ds(V_A, PAIR)].astype(BF16)
        kbd = _block_diag(jnp.concatenate([kprev_ref[...], kcur], axis=0))
        vbd = _block_diag(jnp.concatenate([vprev_ref[...], vcur], axis=0))
        kprev_ref[...] = kcur
        vprev_ref[...] = vcur
        first = first_tile if c == 0 else 0
        for p in range(ATT_PAIRS):
            qa = proj_ref[rows, pl.ds(Q_A + p * PAIR, PAIR)].astype(BF16)
            s = _dot_nt(qa, kbd) + bias_ref[first, p]
            es, inv = [], []
            for j in range(2):
                sj = s[:, j * 2 * CHUNK:(j + 1) * 2 * CHUNK]
                sink = sink_ref[ATT_HEAD_ORDER[2 * p + j]]
                m = jnp.maximum(jnp.max(sj, axis=-1, keepdims=True), sink)
                e = jnp.exp(sj - m)
                es.append(e.astype(BF16))
                inv.append(1.0 / (jnp.sum(e, axis=-1, keepdims=True) + jnp.exp(sink - m)))
            o = _dot(jnp.concatenate(es, axis=1), vbd)
            o = o * jnp.where(_left_half(o.shape), inv[0], inv[1])
            mixed_ref[rows, pl.ds(MIX_ATT + p * PAIR, PAIR)] = o.astype(BF16)

    o_ref[...] = x + _dot(mixed_ref[...], w_out_ref[...])


def _ffn_kernel(x_ref, g2_ref, w_in_ref, w_out_ref, gf_ref, o_ref, act_ref, *, final_norm):
    x = x_ref[...]
    h = _rms_norm(x, g2_ref[...]).astype(BF16)
    for j in range(D_FF // FF_BLOCK):
        a = _dot(h, w_in_ref[:, pl.ds(j * FF_BLOCK, FF_BLOCK)])
        b = _dot(h, w_in_ref[:, pl.ds(D_FF + j * FF_BLOCK, FF_BLOCK)])
        act_ref[:, pl.ds(j * FF_BLOCK, FF_BLOCK)] = (jax.nn.silu(a) * b).astype(BF16)
    y = x + _dot(act_ref[...], w_out_ref[...])
    if final_norm:
        y = _rms_norm(y, gf_ref[...])
    o_ref[...] = y


def _const_spec(shape):
    zeros = (0,) * len(shape)
    return pl.BlockSpec(shape, lambda *_: zeros, pipeline_mode=pl.Buffered(1))


def _mixer_call(x, g1, w_in, w_out, ret_g, ln_g, ln_b, ws, bs, sinks, tables):
    batch, seq, _ = x.shape
    intra_tab, head_tab, tail_tab, sdec_tab, bias_tab = tables
    x_spec = pl.BlockSpec((None, MIX_TILE, D_MODEL), lambda b, t: (b, t, 0))
    consts = (g1, w_in, w_out, ret_g, ln_g, ln_b, ws, bs)
    tabs = (intra_tab, head_tab, tail_tab, sdec_tab, bias_tab)
    return pl.pallas_call(
        _mixer_kernel,
        grid=(batch, seq // MIX_TILE),
        in_specs=[x_spec] + [_const_spec(a.shape) for a in consts]
        + [pl.BlockSpec(memory_space=pltpu.SMEM)] + [_const_spec(a.shape) for a in tabs],
        out_specs=x_spec,
        out_shape=jax.ShapeDtypeStruct(x.shape, x.dtype),
        scratch_shapes=[
            pltpu.VMEM((MIX_TILE, IN_WIDTH), F32),
            pltpu.VMEM((MIX_TILE, D_MODEL), BF16),
            pltpu.VMEM((RET_PAIRS, PAIR, PAIR), F32),
            pltpu.VMEM((CHUNK, PAIR), BF16),
            pltpu.VMEM((CHUNK, PAIR), BF16),
        ],
        compiler_params=pltpu.CompilerParams(
            dimension_semantics=("arbitrary", "arbitrary"),
            vmem_limit_bytes=MIX_VMEM_BYTES),
        name="mixer",
    )(x, *consts, sinks, *tabs)


def _ffn_call(x, g2, w_in, w_out, gf, final_norm):
    tokens = x.shape[0]
    x_spec = pl.BlockSpec((FFN_TILE, D_MODEL), lambda i: (i, 0))
    consts = (g2, w_in, w_out, gf)
    return pl.pallas_call(
        functools.partial(_ffn_kernel, final_norm=final_norm),
        grid=(tokens // FFN_TILE,),
        in_specs=[x_spec] + [_const_spec(a.shape) for a in consts],
        out_specs=x_spec,
        out_shape=jax.ShapeDtypeStruct(x.shape, x.dtype),
        scratch_shapes=[pltpu.VMEM((FFN_TILE, D_FF), BF16)],
        compiler_params=pltpu.CompilerParams(
            dimension_semantics=("arbitrary",),
            vmem_limit_bytes=FFN_VMEM_BYTES),
        name="ffn",
    )(x, *consts)


def _alibi_slopes(n):
    def pow2(m):
        start = 2.0 ** (-(2.0 ** -(math.log2(m) - 3)))
        return [start * start ** i for i in range(m)]
    if math.log2(n).is_integer():
        s = pow2(n)
    else:
        c = 2 ** int(math.floor(math.log2(n)))
        s = pow2(c) + pow2(2 * c)[0::2][: n - c]
    return np.array(s, dtype=np.float32)


def _tables():
    scale = np.float32(HEAD_DIM ** -0.5)
    log_g = jnp.log1p(-(2.0 ** (-RET_DECAY_BASE - jnp.arange(RET_HEADS, dtype=F32))))
    pos = jnp.arange(CHUNK, dtype=F32)
    diff = pos[:, None] - pos[None, :]
    intra = jnp.where(diff[None] >= 0,
                      jnp.exp(log_g[:, None, None] * jnp.maximum(diff, 0.0)[None]), 0.0) * scale
    intra_tab = intra.reshape(RET_PAIRS, 2, CHUNK, CHUNK).transpose(0, 2, 1, 3).reshape(
        RET_PAIRS, CHUNK, 2 * CHUNK)
    lane_gamma = jnp.repeat(log_g, HEAD_DIM).reshape(RET_PAIRS, 1, PAIR)
    head_tab = jnp.exp(lane_gamma * (pos + 1.0)[None, :, None])
    tail_tab = jnp.exp(lane_gamma * (CHUNK - 1.0 - pos)[None, :, None]) * scale
    sdec_tab = jnp.broadcast_to(
        jnp.exp(lane_gamma * CHUNK).reshape(RET_PAIRS, PAIR, 1), (RET_PAIRS, PAIR, PAIR))

    qi = jnp.arange(CHUNK)
    kj = jnp.arange(2 * CHUNK)
    dist = CHUNK + qi[:, None] - kj[None, :]
    in_window = (dist >= 0) & (dist < WINDOW)
    allowed = jnp.stack([in_window, in_window & (kj[None, :] >= CHUNK)])
    slopes = jnp.asarray(_alibi_slopes(ATT_HEADS))[jnp.asarray(ATT_HEAD_ORDER)]
    bias = jnp.where(allowed[:, None], -(slopes[None, :, None, None] * dist.astype(F32)),
                     -jnp.inf)
    bias_tab = bias.reshape(2, ATT_PAIRS, 2, CHUNK, 2 * CHUNK).transpose(0, 1, 3, 2, 4).reshape(
        2, ATT_PAIRS, CHUNK, 4 * CHUNK)
    return intra_tab, head_tab, tail_tab, sdec_tab, bias_tab


def kernel(x, norm1_g, w_in, ret_norm_g, mlp_ln_g, mlp_ln_b, w_spatial, b_spatial, attn_sinks,
           w_out, norm2_g, w_ffn_in, w_ffn_out, final_norm_g):
    batch, seq, d_model = x.shape
    depth = w_in.shape[0]
    assert d_model == D_MODEL and seq % MIX_TILE == 0 and (batch * seq) % FFN_TILE == 0

    heads = [slice(h * HEAD_DIM, (h + 1) * HEAD_DIM) for h in ATT_HEAD_ORDER]
    w_in_b = jnp.concatenate(
        [w_in[:, :, :Q_A]] + [w_in[:, :, Q_A:K_A][:, :, h] for h in heads] + [w_in[:, :, K_A:]],
        axis=2).astype(BF16)
    w_out_b = jnp.concatenate(
        [w_out[:, :MIX_ATT]] + [w_out[:, MIX_ATT:][:, h] for h in heads], axis=1).astype(BF16)
    w_ffn_in_b = w_ffn_in.astype(BF16)
    w_ffn_out_b = w_ffn_out.astype(BF16)
    bs_tab = jnp.repeat(jnp.swapaxes(b_spatial, 1, 2), HEAD_DIM, axis=2)
    tables = _tables()
    row = lambda a: a.reshape(1, -1)

    for l in range(depth):
        x = _mixer_call(x, row(norm1_g[l]), w_in_b[l], w_out_b[l], row(ret_norm_g[l]),
                        row(mlp_ln_g[l]), row(mlp_ln_b[l]), w_spatial[l], bs_tab[l],
                        attn_sinks[l], tables)
        x = _ffn_call(x.reshape(batch * seq, d_model), row(norm2_g[l]), w_ffn_in_b[l],
                      w_ffn_out_b[l], row(final_norm_g), final_norm=(l == depth - 1))
        x = x.reshape(batch, seq, d_model)
    return x
```

```python
import functools
import math

import jax
import jax.numpy as jnp
import numpy as np
from jax import lax
from jax.experimental import pallas as pl
from jax.experimental.pallas import tpu as pltpu

D_MODEL = 1024
HEAD_DIM = 64
CHUNK = 128
RET_HEADS = 6
RET_WIDTH = RET_HEADS * HEAD_DIM
MLP_GROUPS = 4
MLP_WIDTH = MLP_GROUPS * HEAD_DIM
ATT_HEADS = 6
ATT_KV_HEADS = 2
ATT_WIDTH = ATT_HEADS * HEAD_DIM
ATT_KV_WIDTH = ATT_KV_HEADS * HEAD_DIM
WINDOW = 128
IN_WIDTH = 4 * RET_WIDTH + 2 * MLP_WIDTH + ATT_WIDTH + 2 * ATT_KV_WIDTH
D_FF = -(-8 * D_MODEL // (3 * 256)) * 256
RET_DECAY_BASE = 5.0
NORM_EPS = 1e-6

LANES = 128
PAIR = 2 * HEAD_DIM
assert PAIR == LANES and CHUNK == LANES and WINDOW == CHUNK

Q_R, K_R, V_R, G_R = 0, RET_WIDTH, 2 * RET_WIDTH, 3 * RET_WIDTH
Z_M = 4 * RET_WIDTH
Q_A = Z_M + 2 * MLP_WIDTH
K_A = Q_A + ATT_WIDTH
V_A = K_A + ATT_KV_WIDTH
MIX_RET, MIX_MLP, MIX_ATT = 0, RET_WIDTH, RET_WIDTH + MLP_WIDTH

RET_PAIRS = RET_HEADS // 2
MLP_PAIRS = MLP_GROUPS // 2
ATT_PAIRS = ATT_HEADS // 2
ATT_GROUP = ATT_HEADS // ATT_KV_HEADS
ATT_HEAD_ORDER = tuple(h for p in range(ATT_PAIRS) for h in (p, p + ATT_GROUP))

MIX_TILE = 512
MIX_CHUNKS = MIX_TILE // CHUNK
PROJ_BLOCK = 512
FFN_TILE = 512
FF_BLOCK = 256
MIX_VMEM_BYTES = 52 * 1024 * 1024
FFN_VMEM_BYTES = 56 * 1024 * 1024

BF16 = jnp.bfloat16
F32 = jnp.float32


def _dot(a, b):
    return jnp.dot(a, b, preferred_element_type=F32)


def _dot_nt(a, b):
    return lax.dot_general(a, b, (((1,), (1,)), ((), ())), preferred_element_type=F32)


def _left_half(shape):
    return lax.broadcasted_iota(jnp.int32, shape, len(shape) - 1) < HEAD_DIM


def _block_diag(pair):
    left = _left_half(pair.shape)
    zero = jnp.zeros_like(pair)
    return jnp.concatenate([jnp.where(left, pair, zero), jnp.where(left, zero, pair)], axis=0)


def _rms_norm(x, g):
    ms = jnp.mean(x * x, axis=-1, keepdims=True)
    return x * lax.rsqrt(ms + NORM_EPS) * g


def _gelu(x):
    return 0.5 * x * (1.0 + lax.erf(x * np.float32(math.sqrt(0.5))))


def _col_blocks(width, block):
    return [(c, min(block, width - c)) for c in range(0, width, block)]


def _in_proj_jobs(x_rows, g1, w_in_ref, proj_ref, rows):
    h = []

    def piece(col, width):
        if not h:
            h.append(_rms_norm(x_rows(), g1).astype(BF16))
        proj_ref[rows, pl.ds(col, width)] = _dot(h[0], w_in_ref[:, pl.ds(col, width)])

    return [functools.partial(piece, c, w) for c, w in _col_blocks(IN_WIDTH, PROJ_BLOCK)]


def _out_proj_jobs(x_ref, mixed_ref, w_out_ref, o_ref, rows):
    def piece(col, width):
        cols = pl.ds(col, width)
        o_ref[rows, cols] = x_ref[rows, cols] + _dot(mixed_ref[rows, :], w_out_ref[:, cols])

    return [functools.partial(piece, c, w) for c, w in _col_blocks(D_MODEL, PROJ_BLOCK)]


def _retention_scores(p, rows, proj_ref, state_ref, head_ref, tail_ref, sdec_ref):
    q = proj_ref[rows, pl.ds(Q_R + p * PAIR, PAIR)]
    k = proj_ref[rows, pl.ds(K_R + p * PAIR, PAIR)]
    vb = proj_ref[rows, pl.ds(V_R + p * PAIR, PAIR)].astype(BF16)
    kt = (k * tail_ref[p]).T.astype(BF16)
    scores = _dot_nt(q.astype(BF16), _block_diag(k.astype(BF16)))
    state = state_ref[p]
    cross = _dot((q * head_ref[p]).astype(BF16), state.astype(BF16))
    kv = _dot(kt, vb)
    same_head = _left_half((PAIR, PAIR)) == (
        lax.broadcasted_iota(jnp.int32, (PAIR, PAIR), 0) < HEAD_DIM)
    state_ref[p] = state * sdec_ref[p] + jnp.where(same_head, kv, 0.0)
    return scores, cross, vb


def _retention_out(p, rows, proj_ref, out, ret_g):
    sq = out * out
    lh = _left_half(out.shape)
    ss_l = jnp.sum(jnp.where(lh, sq, 0.0), axis=-1, keepdims=True)
    ss_r = jnp.sum(jnp.where(lh, 0.0, sq), axis=-1, keepdims=True)
    ms = jnp.where(lh, ss_l, ss_r) * np.float32(1.0 / HEAD_DIM)
    ret = out * lax.rsqrt(ms + NORM_EPS) * ret_g[:, p * PAIR:(p + 1) * PAIR]
    g = proj_ref[rows, pl.ds(G_R + p * PAIR, PAIR)]
    return jax.nn.silu(g) * ret


def _softmax_pair(p, s, sink_ref, layer):
    es, inv = [], []
    for j in range(2):
        sj = s[:, j * 2 * CHUNK:(j + 1) * 2 * CHUNK]
        sink = sink_ref[layer, ATT_HEAD_ORDER[2 * p + j]]
        m = jnp.maximum(jnp.max(sj, axis=-1, keepdims=True), sink)
        e = jnp.exp(sj - m)
        es.append(e.astype(BF16))
        inv.append(1.0 / (jnp.sum(e, axis=-1, keepdims=True) + jnp.exp(sink - m)))
    return jnp.concatenate(es, axis=1), jnp.where(_left_half((CHUNK, PAIR)), inv[0], inv[1])


def _mixer_kernel(x_ref, xn_ref, g1_ref, w_in_ref, w_out_ref, ret_g_ref, ln_g_ref, ln_b_ref,
                  ws_ref, bs_ref, sink_ref, intra_ref, head_ref, tail_ref, sdec_ref, bias_ref,
                  o_ref, proj_ref, mixed_ref, wsb_ref, state_ref, kprev_ref, vprev_ref,
                  *, layer, tiles_per_seq):
    step = pl.program_id(0)
    g1 = g1_ref[...]
    chunk_rows = [pl.ds(c * CHUNK, CHUNK) for c in range(MIX_CHUNKS)]

    @pl.when(step == 0)
    def _():
        causal = lax.broadcasted_iota(jnp.int32, (CHUNK, CHUNK), 0) >= lax.broadcasted_iota(
            jnp.int32, (CHUNK, CHUNK), 1)
        for g in range(MLP_GROUPS):
            wsb_ref[:, pl.ds(g * CHUNK, CHUNK)] = jnp.where(causal, ws_ref[g], 0.0).astype(BF16)
        for job in _in_proj_jobs(lambda: x_ref[chunk_rows[0], :], g1, w_in_ref, proj_ref,
                                 chunk_rows[0]):
            job()

    seq_start = step % tiles_per_seq == 0

    @pl.when(seq_start)
    def _():
        state_ref[...] = jnp.zeros_like(state_ref)
        kprev_ref[...] = jnp.zeros_like(kprev_ref)
        vprev_ref[...] = jnp.zeros_like(vprev_ref)

    ret_g = ret_g_ref[...]
    ln_g = ln_g_ref[...]
    ln_b = ln_b_ref[...]
    first_block = seq_start.astype(jnp.int32)

    for c in range(MIX_CHUNKS):
        rows = chunk_rows[c]
        if c + 1 < MIX_CHUNKS:
            jobs = _in_proj_jobs(lambda c=c: x_ref[chunk_rows[c + 1], :], g1, w_in_ref, proj_ref,
                                 chunk_rows[c + 1])
        else:
            jobs = _in_proj_jobs(lambda: xn_ref[...], g1, w_in_ref, proj_ref, chunk_rows[0])
        if c > 0:
            jobs += _out_proj_jobs(x_ref, mixed_ref, w_out_ref, o_ref, chunk_rows[c - 1])
        jobs = iter(jobs)

        def fill():
            job = next(jobs, None)
            if job is not None:
                job()

        ret1 = [_retention_scores(p, rows, proj_ref, state_ref, head_ref, tail_ref, sdec_ref)
                for p in range(RET_PAIRS)]
        fill()

        kcur = (proj_ref[rows, pl.ds(K_A, PAIR)] * np.float32(HEAD_DIM ** -0.5)).astype(BF16)
        vcur = proj_ref[rows, pl.ds(V_A, PAIR)].astype(BF16)
        kbd = _block_diag(jnp.concatenate([kprev_ref[...], kcur], axis=0))
        vbd = _block_diag(jnp.concatenate([vprev_ref[...], vcur], axis=0))
        kprev_ref[...] = kcur
        vprev_ref[...] = vcur
        first = first_block if c == 0 else 0
        att_s = [_dot_nt(proj_ref[rows, pl.ds(Q_A + p * PAIR, PAIR)].astype(BF16), kbd)
                 + bias_ref[first, p] for p in range(ATT_PAIRS)]
        z = _gelu(proj_ref[rows, pl.ds(Z_M, 2 * MLP_WIDTH)])
        u = z[:, :MLP_WIDTH]
        v = z[:, MLP_WIDTH:]
        mu = jnp.mean(v, axis=-1, keepdims=True)
        var = jnp.mean(jnp.square(v - mu), axis=-1, keepdims=True)
        vn = ((v - mu) * lax.rsqrt(var + NORM_EPS) * ln_g + ln_b).astype(BF16)
        fill()

        ret_out = [_dot((scores * intra_ref[p]).astype(BF16), _block_diag(vb)) + cross
                   for p, (scores, cross, vb) in enumerate(ret1)]
        fill()

        for p in range(MLP_PAIRS):
            cols = slice(p * PAIR, (p + 1) * PAIR)
            gate = _dot(wsb_ref[:, pl.ds(2 * p * CHUNK, 2 * CHUNK)], _block_diag(vn[:, cols]))
            mixed_ref[rows, pl.ds(MIX_MLP + p * PAIR, PAIR)] = (
                u[:, cols] * (gate + bs_ref[:, cols])).astype(BF16)
        att_p = [_softmax_pair(p, att_s[p], sink_ref, layer) for p in range(ATT_PAIRS)]
        fill()

        att_o = [_dot(e, vbd) for e, _ in att_p]
        fill()

        for p in range(RET_PAIRS):
            mixed_ref[rows, pl.ds(MIX_RET + p * PAIR, PAIR)] = _retention_out(
                p, rows, proj_ref, ret_out[p], ret_g).astype(BF16)
        fill()

        for p in range(ATT_PAIRS):
            mixed_ref[rows, pl.ds(MIX_ATT + p * PAIR, PAIR)] = (att_o[p] * att_p[p][1]).astype(BF16)
        fill()
        fill()
        assert next(jobs, None) is None

    for job in _out_proj_jobs(x_ref, mixed_ref, w_out_ref, o_ref, chunk_rows[MIX_CHUNKS - 1]):
        job()


def _ffn_kernel(x_ref, g2_ref, w_in_ref, w_out_ref, gf_ref, o_ref, act_ref, *, final_norm):
    x = x_ref[...]
    h = _rms_norm(x, g2_ref[...]).astype(BF16)
    for j in range(D_FF // FF_BLOCK):
        a = _dot(h, w_in_ref[:, pl.ds(j * FF_BLOCK, FF_BLOCK)])
        b = _dot(h, w_in_ref[:, pl.ds(D_FF + j * FF_BLOCK, FF_BLOCK)])
        act_ref[:, pl.ds(j * FF_BLOCK, FF_BLOCK)] = (jax.nn.silu(a) * b).astype(BF16)
    y = x + _dot(act_ref[...], w_out_ref[...])
    if final_norm:
        y = _rms_norm(y, gf_ref[...])
    o_ref[...] = y


def _layer_spec(stacked, layer):
    tail = (0,) * (stacked.ndim - 1)
    return pl.BlockSpec((None,) + stacked.shape[1:], lambda *_: (layer,) + tail,
                        pipeline_mode=pl.Buffered(1))


def _const_spec(arr):
    zeros = (0,) * arr.ndim
    return pl.BlockSpec(arr.shape, lambda *_: zeros, pipeline_mode=pl.Buffered(1))


def _mixer_call(x, layer, seq, g1, w_in, w_out, ret_g, ln_g, ln_b, ws, bs, sinks, tables):
    tokens = x.shape[0]
    n_tiles = tokens // MIX_TILE
    last_chunk = tokens // CHUNK - 1
    x_spec = pl.BlockSpec((MIX_TILE, D_MODEL), lambda s: (s, 0))
    next_spec = pl.BlockSpec(
        (CHUNK, D_MODEL), lambda s: (jnp.minimum((s + 1) * MIX_CHUNKS, last_chunk), 0))
    stacked = (g1, w_in, w_out, ret_g, ln_g, ln_b, ws, bs)
    return pl.pallas_call(
        functools.partial(_mixer_kernel, layer=layer, tiles_per_seq=seq // MIX_TILE),
        grid=(n_tiles,),
        in_specs=[x_spec, next_spec] + [_layer_spec(a, layer) for a in stacked]
        + [pl.BlockSpec(memory_space=pltpu.SMEM)] + [_const_spec(a) for a in tables],
        out_specs=x_spec,
        out_shape=jax.ShapeDtypeStruct(x.shape, x.dtype),
        scratch_shapes=[
            pltpu.VMEM((MIX_TILE, IN_WIDTH), F32),
            pltpu.VMEM((MIX_TILE, D_MODEL), BF16),
            pltpu.VMEM((CHUNK, MLP_GROUPS * CHUNK), BF16),
            pltpu.VMEM((RET_PAIRS, PAIR, PAIR), F32),
            pltpu.VMEM((CHUNK, PAIR), BF16),
            pltpu.VMEM((CHUNK, PAIR), BF16),
        ],
        compiler_params=pltpu.CompilerParams(
            dimension_semantics=("arbitrary",),
            vmem_limit_bytes=MIX_VMEM_BYTES),
        name="mixer",
    )(x, x, *stacked, sinks, *tables)


def _ffn_call(x, layer, g2, w_in, w_out, gf, final_norm):
    tokens = x.shape[0]
    x_spec = pl.BlockSpec((FFN_TILE, D_MODEL), lambda i: (i, 0))
    stacked = (g2, w_in, w_out)
    return pl.pallas_call(
        functools.partial(_ffn_kernel, final_norm=final_norm),
        grid=(tokens // FFN_TILE,),
        in_specs=[x_spec] + [_layer_spec(a, layer) for a in stacked] + [_const_spec(gf)],
        out_specs=x_spec,
        out_shape=jax.ShapeDtypeStruct(x.shape, x.dtype),
        scratch_shapes=[pltpu.VMEM((FFN_TILE, D_FF), BF16)],
        compiler_params=pltpu.CompilerParams(
            dimension_semantics=("arbitrary",),
            vmem_limit_bytes=FFN_VMEM_BYTES),
        name="ffn",
    )(x, *stacked, gf)


def _alibi_slopes(n):
    def pow2(m):
        start = 2.0 ** (-(2.0 ** -(math.log2(m) - 3)))
        return [start * start ** i for i in range(m)]
    if math.log2(n).is_integer():
        s = pow2(n)
    else:
        c = 2 ** int(math.floor(math.log2(n)))
        s = pow2(c) + pow2(2 * c)[0::2][: n - c]
    return np.array(s, dtype=np.float32)


def _tables():
    scale = np.float32(HEAD_DIM ** -0.5)
    log_g = jnp.log1p(-(2.0 ** (-RET_DECAY_BASE - jnp.arange(RET_HEADS, dtype=F32))))
    pos = jnp.arange(CHUNK, dtype=F32)
    diff = pos[:, None] - pos[None, :]
    intra = jnp.where(diff[None] >= 0,
                      jnp.exp(log_g[:, None, None] * jnp.maximum(diff, 0.0)[None]), 0.0) * scale
    intra_tab = intra.reshape(RET_PAIRS, 2, CHUNK, CHUNK).transpose(0, 2, 1, 3).reshape(
        RET_PAIRS, CHUNK, 2 * CHUNK)
    lane_gamma = jnp.repeat(log_g, HEAD_DIM).reshape(RET_PAIRS, 1, PAIR)
    head_tab = jnp.exp(lane_gamma * (pos + 1.0)[None, :, None])
    tail_tab = jnp.exp(lane_gamma * (CHUNK - 1.0 - pos)[None, :, None]) * scale
    sdec_tab = jnp.broadcast_to(
        jnp.exp(lane_gamma * CHUNK).reshape(RET_PAIRS, PAIR, 1), (RET_PAIRS, PAIR, PAIR))

    qi = jnp.arange(CHUNK)
    kj = jnp.arange(2 * CHUNK)
    dist = CHUNK + qi[:, None] - kj[None, :]
    in_window = (dist >= 0) & (dist < WINDOW)
    allowed = jnp.stack([in_window, in_window & (kj[None, :] >= CHUNK)])
    slopes = jnp.asarray(_alibi_slopes(ATT_HEADS))[jnp.asarray(ATT_HEAD_ORDER)]
    bias = jnp.where(allowed[:, None], -(slopes[None, :, None, None] * dist.astype(F32)),
                     -jnp.inf)
    bias_tab = bias.reshape(2, ATT_PAIRS, 2, CHUNK, 2 * CHUNK).transpose(0, 1, 3, 2, 4).reshape(
        2, ATT_PAIRS, CHUNK, 4 * CHUNK)
    return intra_tab, head_tab, tail_tab, sdec_tab, bias_tab


def kernel(x, norm1_g, w_in, ret_norm_g, mlp_ln_g, mlp_ln_b, w_spatial, b_spatial, attn_sinks,
           w_out, norm2_g, w_ffn_in, w_ffn_out, final_norm_g):
    batch, seq, d_model = x.shape
    depth = w_in.shape[0]
    tokens = batch * seq
    assert d_model == D_MODEL and seq % MIX_TILE == 0 and tokens % FFN_TILE == 0

    heads = [slice(h * HEAD_DIM, (h + 1) * HEAD_DIM) for h in ATT_HEAD_ORDER]
    w_in_b = jnp.concatenate(
        [w_in[:, :, :Q_A]] + [w_in[:, :, Q_A:K_A][:, :, h] for h in heads] + [w_in[:, :, K_A:]],
        axis=2).astype(BF16)
    w_out_b = jnp.concatenate(
        [w_out[:, :MIX_ATT]] + [w_out[:, MIX_ATT:][:, h] for h in heads], axis=1).astype(BF16)
    w_ffn_in_b = w_ffn_in.astype(BF16)
    w_ffn_out_b = w_ffn_out.astype(BF16)
    bs_tab = jnp.repeat(jnp.swapaxes(b_spatial, 1, 2), HEAD_DIM, axis=2)
    tables = _tables()
    rows = lambda a: a.reshape(depth, 1, -1)

    x = x.reshape(tokens, d_model)
    for l in range(depth):
        x = _mixer_call(x, l, seq, rows(norm1_g), w_in_b, w_out_b, rows(ret_norm_g),
                        rows(mlp_ln_g), rows(mlp_ln_b), w_spatial, bs_tab, attn_sinks, tables)
        x = _ffn_call(x, l, rows(norm2_g), w_ffn_in_b, w_ffn_out_b, final_norm_g.reshape(1, -1),
                      final_norm=(l == depth - 1))
    return x.reshape(batch, seq, d_model)
```

```python
import functools
import math

import jax
import jax.numpy as jnp
import numpy as np
from jax import lax
from jax.experimental import pallas as pl
from jax.experimental.pallas import tpu as pltpu

D_MODEL = 1024
HEAD_DIM = 64
CHUNK = 128
RET_HEADS = 6
RET_WIDTH = RET_HEADS * HEAD_DIM
MLP_GROUPS = 4
MLP_WIDTH = MLP_GROUPS * HEAD_DIM
ATT_HEADS = 6
ATT_KV_HEADS = 2
ATT_WIDTH = ATT_HEADS * HEAD_DIM
ATT_KV_WIDTH = ATT_KV_HEADS * HEAD_DIM
WINDOW = 128
IN_WIDTH = 4 * RET_WIDTH + 2 * MLP_WIDTH + ATT_WIDTH + 2 * ATT_KV_WIDTH
D_FF = -(-8 * D_MODEL // (3 * 256)) * 256
RET_DECAY_BASE = 5.0
NORM_EPS = 1e-6

LANES = 128
PAIR = 2 * HEAD_DIM
assert PAIR == LANES and CHUNK == LANES and WINDOW == CHUNK

Q_R, K_R, V_R, G_R = 0, RET_WIDTH, 2 * RET_WIDTH, 3 * RET_WIDTH
Z_M = 4 * RET_WIDTH
Q_A = Z_M + 2 * MLP_WIDTH
K_A = Q_A + ATT_WIDTH
V_A = K_A + ATT_KV_WIDTH
MIX_RET, MIX_MLP, MIX_ATT = 0, RET_WIDTH, RET_WIDTH + MLP_WIDTH

RET_PAIRS = RET_HEADS // 2
MLP_PAIRS = MLP_GROUPS // 2
ATT_PAIRS = ATT_HEADS // 2
ATT_GROUP = ATT_HEADS // ATT_KV_HEADS
ATT_HEAD_ORDER = tuple(h for p in range(ATT_PAIRS) for h in (p, p + ATT_GROUP))

MIX_TILE = 512
MIX_CHUNKS = MIX_TILE // CHUNK
PROJ_BLOCK = 512
FFN_TILE = 512
FF_BLOCK = 256
MIX_VMEM_BYTES = 52 * 1024 * 1024
FFN_VMEM_BYTES = 56 * 1024 * 1024

BF16 = jnp.bfloat16
F32 = jnp.float32


def _dot(a, b):
    return jnp.dot(a, b, preferred_element_type=F32)


def _left_half(shape):
    return lax.broadcasted_iota(jnp.int32, shape, len(shape) - 1) < HEAD_DIM


def _block_diag(pair):
    left = _left_half(pair.shape)
    zero = jnp.zeros_like(pair)
    return jnp.concatenate([jnp.where(left, pair, zero), jnp.where(left, zero, pair)], axis=0)


def _block_diag_t(pair_t):
    top = lax.broadcasted_iota(jnp.int32, pair_t.shape, 0) < HEAD_DIM
    zero = jnp.zeros_like(pair_t)
    return jnp.concatenate([jnp.where(top, pair_t, zero), jnp.where(top, zero, pair_t)], axis=1)


def _rms_norm(x, g):
    ms = jnp.mean(x * x, axis=-1, keepdims=True)
    return x * lax.rsqrt(ms + NORM_EPS) * g


def _gelu(x):
    return 0.5 * x * (1.0 + lax.erf(x * np.float32(math.sqrt(0.5))))


def _col_blocks(width, block):
    return [(c, min(block, width - c)) for c in range(0, width, block)]


def _in_proj_jobs(x_rows, g1, w_in_ref, proj_ref, rows):
    h = []

    def piece(col, width):
        if not h:
            h.append(_rms_norm(x_rows(), g1).astype(BF16))
        proj_ref[rows, pl.ds(col, width)] = _dot(h[0], w_in_ref[:, pl.ds(col, width)])

    return [functools.partial(piece, c, w) for c, w in _col_blocks(IN_WIDTH, PROJ_BLOCK)]


def _out_proj_jobs(x_ref, mixed_ref, w_out_ref, o_ref, rows):
    def piece(col, width):
        cols = pl.ds(col, width)
        o_ref[rows, cols] = x_ref[rows, cols] + _dot(mixed_ref[rows, :], w_out_ref[:, cols])

    return [functools.partial(piece, c, w) for c, w in _col_blocks(D_MODEL, PROJ_BLOCK)]


def _retention_scores(p, rows, proj_ref, state_ref, head_ref, tail_ref, sdec_ref):
    q = proj_ref[rows, pl.ds(Q_R + p * PAIR, PAIR)]
    k = proj_ref[rows, pl.ds(K_R + p * PAIR, PAIR)]
    vb = proj_ref[rows, pl.ds(V_R + p * PAIR, PAIR)].astype(BF16)
    kt = k.T
    scores = _dot(q.astype(BF16), _block_diag_t(kt.astype(BF16)))
    state = state_ref[p]
    cross = _dot((q * head_ref[p]).astype(BF16), state.astype(BF16))
    kv = _dot((kt * tail_ref[p]).astype(BF16), vb)
    same_head = _left_half((PAIR, PAIR)) == (
        lax.broadcasted_iota(jnp.int32, (PAIR, PAIR), 0) < HEAD_DIM)
    state_ref[p] = state * sdec_ref[p] + jnp.where(same_head, kv, 0.0)
    return scores, cross, vb


def _retention_out(p, rows, proj_ref, out, ret_g):
    sq = out * out
    lh = _left_half(out.shape)
    ss_l = jnp.sum(jnp.where(lh, sq, 0.0), axis=-1, keepdims=True)
    ss_r = jnp.sum(jnp.where(lh, 0.0, sq), axis=-1, keepdims=True)
    ms = jnp.where(lh, ss_l, ss_r) * np.float32(1.0 / HEAD_DIM)
    ret = out * lax.rsqrt(ms + NORM_EPS) * ret_g[:, p * PAIR:(p + 1) * PAIR]
    g = proj_ref[rows, pl.ds(G_R + p * PAIR, PAIR)]
    return jax.nn.silu(g) * ret


def _softmax_pair(p, s, sink_ref, layer):
    es, inv = [], []
    for j in range(2):
        sj = s[:, j * 2 * CHUNK:(j + 1) * 2 * CHUNK]
        sink = sink_ref[layer, ATT_HEAD_ORDER[2 * p + j]]
        m = jnp.maximum(jnp.max(sj, axis=-1, keepdims=True), sink)
        e = jnp.exp(sj - m)
        es.append(e.astype(BF16))
        inv.append(1.0 / (jnp.sum(e, axis=-1, keepdims=True) + jnp.exp(sink - m)))
    return jnp.concatenate(es, axis=1), jnp.where(_left_half((CHUNK, PAIR)), inv[0], inv[1])


def _mixer_kernel(x_ref, xn_ref, g1_ref, w_in_ref, w_out_ref, ret_g_ref, ln_g_ref, ln_b_ref,
                  ws_ref, bs_ref, sink_ref, intra_ref, head_ref, tail_ref, sdec_ref, bias_ref,
                  o_ref, proj_ref, mixed_ref, wsb_ref, state_ref, kprev_ref, vprev_ref,
                  *, layer, tiles_per_seq):
    step = pl.program_id(0)
    g1 = g1_ref[...]
    chunk_rows = [pl.ds(c * CHUNK, CHUNK) for c in range(MIX_CHUNKS)]

    @pl.when(step == 0)
    def _():
        causal = lax.broadcasted_iota(jnp.int32, (CHUNK, CHUNK), 0) >= lax.broadcasted_iota(
            jnp.int32, (CHUNK, CHUNK), 1)
        for g in range(MLP_GROUPS):
            wsb_ref[:, pl.ds(g * CHUNK, CHUNK)] = jnp.where(causal, ws_ref[g], 0.0).astype(BF16)
        for job in _in_proj_jobs(lambda: x_ref[chunk_rows[0], :], g1, w_in_ref, proj_ref,
                                 chunk_rows[0]):
            job()

    seq_start = step % tiles_per_seq == 0

    @pl.when(seq_start)
    def _():
        state_ref[...] = jnp.zeros_like(state_ref)
        kprev_ref[...] = jnp.zeros_like(kprev_ref)
        vprev_ref[...] = jnp.zeros_like(vprev_ref)

    ret_g = ret_g_ref[...]
    ln_g = ln_g_ref[...]
    ln_b = ln_b_ref[...]
    first_block = seq_start.astype(jnp.int32)

    for c in range(MIX_CHUNKS):
        rows = chunk_rows[c]
        if c + 1 < MIX_CHUNKS:
            jobs = _in_proj_jobs(lambda c=c: x_ref[chunk_rows[c + 1], :], g1, w_in_ref, proj_ref,
                                 chunk_rows[c + 1])
        else:
            jobs = _in_proj_jobs(lambda: xn_ref[...], g1, w_in_ref, proj_ref, chunk_rows[0])
        if c > 0:
            jobs += _out_proj_jobs(x_ref, mixed_ref, w_out_ref, o_ref, chunk_rows[c - 1])
        jobs = iter(jobs)

        def fill():
            job = next(jobs, None)
            if job is not None:
                job()

        ret1 = [_retention_scores(p, rows, proj_ref, state_ref, head_ref, tail_ref, sdec_ref)
                for p in range(RET_PAIRS)]
        fill()

        kcur = (proj_ref[rows, pl.ds(K_A, PAIR)] * np.float32(HEAD_DIM ** -0.5)).T.astype(BF16)
        vcur = proj_ref[rows, pl.ds(V_A, PAIR)].astype(BF16)
        kbd = _block_diag_t(jnp.concatenate([kprev_ref[...], kcur], axis=1))
        vbd = _block_diag(jnp.concatenate([vprev_ref[...], vcur], axis=0))
        kprev_ref[...] = kcur
        vprev_ref[...] = vcur
        first = first_block if c == 0 else 0
        att_s = [_dot(proj_ref[rows, pl.ds(Q_A + p * PAIR, PAIR)].astype(BF16), kbd)
                 + bias_ref[first, p] for p in range(ATT_PAIRS)]
        z = _gelu(proj_ref[rows, pl.ds(Z_M, 2 * MLP_WIDTH)])
        u = z[:, :MLP_WIDTH]
        v = z[:, MLP_WIDTH:]
        mu = jnp.mean(v, axis=-1, keepdims=True)
        var = jnp.mean(jnp.square(v - mu), axis=-1, keepdims=True)
        vn = ((v - mu) * lax.rsqrt(var + NORM_EPS) * ln_g + ln_b).astype(BF16)
        fill()

        ret_out = [_dot((scores * intra_ref[p]).astype(BF16), _block_diag(vb)) + cross
                   for p, (scores, cross, vb) in enumerate(ret1)]
        fill()

        for p in range(MLP_PAIRS):
            cols = slice(p * PAIR, (p + 1) * PAIR)
            gate = _dot(wsb_ref[:, pl.ds(2 * p * CHUNK, 2 * CHUNK)], _block_diag(vn[:, cols]))
            mixed_ref[rows, pl.ds(MIX_MLP + p * PAIR, PAIR)] = (
                u[:, cols] * (gate + bs_ref[:, cols])).astype(BF16)
        att_p = [_softmax_pair(p, att_s[p], sink_ref, layer) for p in range(ATT_PAIRS)]
        fill()

        att_o = [_dot(e, vbd) for e, _ in att_p]
        fill()

        for p in range(RET_PAIRS):
            mixed_ref[rows, pl.ds(MIX_RET + p * PAIR, PAIR)] = _retention_out(
                p, rows, proj_ref, ret_out[p], ret_g).astype(BF16)
        fill()

        for p in range(ATT_PAIRS):
            mixed_ref[rows, pl.ds(MIX_ATT + p * PAIR, PAIR)] = (att_o[p] * att_p[p][1]).astype(BF16)
        fill()
        fill()
        assert next(jobs, None) is None

    for job in _out_proj_jobs(x_ref, mixed_ref, w_out_ref, o_ref, chunk_rows[MIX_CHUNKS - 1]):
        job()


def _ffn_kernel(x_ref, g2_ref, w_in_ref, w_out_ref, gf_ref, o_ref, act_ref, *, final_norm):
    x = x_ref[...]
    h = _rms_norm(x, g2_ref[...]).astype(BF16)
    for j in range(D_FF // FF_BLOCK):
        a = _dot(h, w_in_ref[:, pl.ds(j * FF_BLOCK, FF_BLOCK)])
        b = _dot(h, w_in_ref[:, pl.ds(D_FF + j * FF_BLOCK, FF_BLOCK)])
        act_ref[:, pl.ds(j * FF_BLOCK, FF_BLOCK)] = (jax.nn.silu(a) * b).astype(BF16)
    y = x + _dot(act_ref[...], w_out_ref[...])
    if final_norm:
        y = _rms_norm(y, gf_ref[...])
    o_ref[...] = y


def _layer_spec(stacked, layer):
    tail = (0,) * (stacked.ndim - 1)
    return pl.BlockSpec((None,) + stacked.shape[1:], lambda *_: (layer,) + tail,
                        pipeline_mode=pl.Buffered(1))


def _const_spec(arr):
    zeros = (0,) * arr.ndim
    return pl.BlockSpec(arr.shape, lambda *_: zeros, pipeline_mode=pl.Buffered(1))


def _mixer_call(x, layer, seq, g1, w_in, w_out, ret_g, ln_g, ln_b, ws, bs, sinks, tables):
    tokens = x.shape[0]
    n_tiles = tokens // MIX_TILE
    last_chunk = tokens // CHUNK - 1
    x_spec = pl.BlockSpec((MIX_TILE, D_MODEL), lambda s: (s, 0))
    next_spec = pl.BlockSpec(
        (CHUNK, D_MODEL), lambda s: (jnp.minimum((s + 1) * MIX_CHUNKS, last_chunk), 0))
    stacked = (g1, w_in, w_out, ret_g, ln_g, ln_b, ws, bs)
    return pl.pallas_call(
        functools.partial(_mixer_kernel, layer=layer, tiles_per_seq=seq // MIX_TILE),
        grid=(n_tiles,),
        in_specs=[x_spec, next_spec] + [_layer_spec(a, layer) for a in stacked]
        + [pl.BlockSpec(memory_space=pltpu.SMEM)] + [_const_spec(a) for a in tables],
        out_specs=x_spec,
        out_shape=jax.ShapeDtypeStruct(x.shape, x.dtype),
        scratch_shapes=[
            pltpu.VMEM((MIX_TILE, IN_WIDTH), F32),
            pltpu.VMEM((MIX_TILE, D_MODEL), BF16),
            pltpu.VMEM((CHUNK, MLP_GROUPS * CHUNK), BF16),
            pltpu.VMEM((RET_PAIRS, PAIR, PAIR), F32),
            pltpu.VMEM((CHUNK, PAIR), BF16),
            pltpu.VMEM((CHUNK, PAIR), BF16),
        ],
        compiler_params=pltpu.CompilerParams(
            dimension_semantics=("arbitrary",),
            vmem_limit_bytes=MIX_VMEM_BYTES),
        name="mixer",
    )(x, x, *stacked, sinks, *tables)


def _ffn_call(x, layer, g2, w_in, w_out, gf, final_norm):
    n_tiles = x.shape[0] // FFN_TILE
    x_spec = pl.BlockSpec((FFN_TILE, D_MODEL), lambda i: (i, 0))
    stacked = (g2, w_in, w_out)
    return pl.pallas_call(
        functools.partial(_ffn_kernel, final_norm=final_norm),
        grid=(n_tiles,),
        in_specs=[x_spec] + [_layer_spec(a, layer) for a in stacked] + [_const_spec(gf)],
        out_specs=x_spec,
        out_shape=jax.ShapeDtypeStruct(x.shape, x.dtype),
        scratch_shapes=[pltpu.VMEM((FFN_TILE, D_FF), BF16)],
        compiler_params=pltpu.CompilerParams(
            dimension_semantics=("arbitrary",),
            vmem_limit_bytes=FFN_VMEM_BYTES),
        name="ffn",
    )(x, *stacked, gf)


def _alibi_slopes(n):
    def pow2(m):
        start = 2.0 ** (-(2.0 ** -(math.log2(m) - 3)))
        return [start * start ** i for i in range(m)]
    if math.log2(n).is_integer():
        s = pow2(n)
    else:
        c = 2 ** int(math.floor(math.log2(n)))
        s = pow2(c) + pow2(2 * c)[0::2][: n - c]
    return np.array(s, dtype=np.float32)


def _tables():
    scale = np.float32(HEAD_DIM ** -0.5)
    log_g = jnp.log1p(-(2.0 ** (-RET_DECAY_BASE - jnp.arange(RET_HEADS, dtype=F32))))
    pos = jnp.arange(CHUNK, dtype=F32)
    diff = pos[:, None] - pos[None, :]
    intra = jnp.where(diff[None] >= 0,
                      jnp.exp(log_g[:, None, None] * jnp.maximum(diff, 0.0)[None]), 0.0) * scale
    intra_tab = intra.reshape(RET_PAIRS, 2, CHUNK, CHUNK).transpose(0, 2, 1, 3).reshape(
        RET_PAIRS, CHUNK, 2 * CHUNK)
    lane_gamma = jnp.repeat(log_g, HEAD_DIM).reshape(RET_PAIRS, 1, PAIR)
    head_tab = jnp.exp(lane_gamma * (pos + 1.0)[None, :, None])
    tail_tab = jnp.swapaxes(jnp.exp(lane_gamma * (CHUNK - 1.0 - pos)[None, :, None]) * scale, 1, 2)
    sdec_tab = jnp.broadcast_to(
        jnp.exp(lane_gamma * CHUNK).reshape(RET_PAIRS, PAIR, 1), (RET_PAIRS, PAIR, PAIR))

    qi = jnp.arange(CHUNK)
    kj = jnp.arange(2 * CHUNK)
    dist = CHUNK + qi[:, None] - kj[None, :]
    in_window = (dist >= 0) & (dist < WINDOW)
    allowed = jnp.stack([in_window, in_window & (kj[None, :] >= CHUNK)])
    slopes = jnp.asarray(_alibi_slopes(ATT_HEADS))[jnp.asarray(ATT_HEAD_ORDER)]
    bias = jnp.where(allowed[:, None], -(slopes[None, :, None, None] * dist.astype(F32)),
                     -jnp.inf)
    bias_tab = bias.reshape(2, ATT_PAIRS, 2, CHUNK, 2 * CHUNK).transpose(0, 1, 3, 2, 4).reshape(
        2, ATT_PAIRS, CHUNK, 4 * CHUNK)
    return intra_tab, head_tab, tail_tab, sdec_tab, bias_tab


def kernel(x, norm1_g, w_in, ret_norm_g, mlp_ln_g, mlp_ln_b, w_spatial, b_spatial, attn_sinks,
           w_out, norm2_g, w_ffn_in, w_ffn_out, final_norm_g):
    batch, seq, d_model = x.shape
    depth = w_in.shape[0]
    tokens = batch * seq
    assert d_model == D_MODEL and seq % MIX_TILE == 0 and tokens % FFN_TILE == 0

    def pair_heads(w, axis):
        shape = w.shape
        w = w.reshape(shape[:axis] + (ATT_KV_HEADS, ATT_GROUP, HEAD_DIM) + shape[axis + 1:])
        return jnp.swapaxes(w, axis, axis + 1).reshape(shape)

    w_in_b = w_in.astype(BF16).at[:, :, Q_A:K_A].set(
        pair_heads(w_in[:, :, Q_A:K_A], 2).astype(BF16))
    w_out_b = w_out.astype(BF16).at[:, MIX_ATT:, :].set(
        pair_heads(w_out[:, MIX_ATT:, :], 1).astype(BF16))
    w_ffn_in_b = w_ffn_in.astype(BF16)
    w_ffn_out_b = w_ffn_out.astype(BF16)
    bs_tab = jnp.repeat(jnp.swapaxes(b_spatial, 1, 2), HEAD_DIM, axis=2)
    tables = _tables()
    rows = lambda a: a.reshape(depth, 1, -1)

    x = x.reshape(tokens, d_model)
    for l in range(depth):
        x = _mixer_call(x, l, seq, rows(norm1_g), w_in_b, w_out_b, rows(ret_norm_g),
                        rows(mlp_ln_g), rows(mlp_ln_b), w_spatial, bs_tab, attn_sinks, tables)
        x = _ffn_call(x, l, rows(norm2_g), w_ffn_in_b, w_ffn_out_b, final_norm_g.reshape(1, -1),
                      final_norm=(l == depth - 1))
    return x.reshape(batch, seq, d_model)
```

```python
import functools
import math

import jax
import jax.numpy as jnp
import numpy as np
from jax import lax
from jax.experimental import pallas as pl
from jax.experimental.pallas import tpu as pltpu

D_MODEL = 1024
HEAD_DIM = 64
CHUNK = 128
RET_HEADS = 6
RET_WIDTH = RET_HEADS * HEAD_DIM
MLP_GROUPS = 4
MLP_WIDTH = MLP_GROUPS * HEAD_DIM
ATT_HEADS = 6
ATT_KV_HEADS = 2
ATT_WIDTH = ATT_HEADS * HEAD_DIM
ATT_KV_WIDTH = ATT_KV_HEADS * HEAD_DIM
WINDOW = 128
IN_WIDTH = 4 * RET_WIDTH + 2 * MLP_WIDTH + ATT_WIDTH + 2 * ATT_KV_WIDTH
D_FF = -(-8 * D_MODEL // (3 * 256)) * 256
RET_DECAY_BASE = 5.0
NORM_EPS = 1e-6

LANES = 128
PAIR = 2 * HEAD_DIM
assert PAIR == LANES and CHUNK == LANES and WINDOW == CHUNK

Q_R, K_R, V_R, G_R = 0, RET_WIDTH, 2 * RET_WIDTH, 3 * RET_WIDTH
Z_M = 4 * RET_WIDTH
Q_A = Z_M + 2 * MLP_WIDTH
K_A = Q_A + ATT_WIDTH
V_A = K_A + ATT_KV_WIDTH
MIX_RET, MIX_MLP, MIX_ATT = 0, RET_WIDTH, RET_WIDTH + MLP_WIDTH

RET_PAIRS = RET_HEADS // 2
MLP_PAIRS = MLP_GROUPS // 2
ATT_PAIRS = ATT_HEADS // 2
ATT_GROUP = ATT_HEADS // ATT_KV_HEADS
ATT_HEAD_ORDER = tuple(h for p in range(ATT_PAIRS) for h in (p, p + ATT_GROUP))

MIX_TILE = 1024
MIX_CHUNKS = MIX_TILE // CHUNK
PROJ_BLOCK = 512
FFN_TILE = 1024
FF_BLOCK = 256
MIX_VMEM_BYTES = 52 * 1024 * 1024
FFN_VMEM_BYTES = 56 * 1024 * 1024

BF16 = jnp.bfloat16
F32 = jnp.float32


def _dot(a, b):
    return jnp.dot(a, b, preferred_element_type=F32)


def _left_half(shape):
    return lax.broadcasted_iota(jnp.int32, shape, len(shape) - 1) < HEAD_DIM


def _block_diag(pair):
    left = _left_half(pair.shape)
    zero = jnp.zeros_like(pair)
    return jnp.concatenate([jnp.where(left, pair, zero), jnp.where(left, zero, pair)], axis=0)


def _block_diag_t(pair_t):
    top = lax.broadcasted_iota(jnp.int32, pair_t.shape, 0) < HEAD_DIM
    zero = jnp.zeros_like(pair_t)
    return jnp.concatenate([jnp.where(top, pair_t, zero), jnp.where(top, zero, pair_t)], axis=1)


def _rms_norm(x, g):
    ms = jnp.mean(x * x, axis=-1, keepdims=True)
    return x * lax.rsqrt(ms + NORM_EPS) * g


def _gelu(x):
    return 0.5 * x * (1.0 + lax.erf(x * np.float32(math.sqrt(0.5))))


def _col_blocks(width, block):
    return [(c, min(block, width - c)) for c in range(0, width, block)]


def _in_proj_jobs(x_rows, g1, w_in_ref, proj_ref, rows):
    h = []

    def piece(col, width):
        if not h:
            h.append(_rms_norm(x_rows(), g1).astype(BF16))
        proj_ref[rows, pl.ds(col, width)] = _dot(h[0], w_in_ref[:, pl.ds(col, width)])

    return [functools.partial(piece, c, w) for c, w in _col_blocks(IN_WIDTH, PROJ_BLOCK)]


def _out_proj_jobs(x_ref, mixed_ref, w_out_ref, o_ref, rows):
    def piece(col, width):
        cols = pl.ds(col, width)
        o_ref[rows, cols] = x_ref[rows, cols] + _dot(mixed_ref[rows, :], w_out_ref[:, cols])

    return [functools.partial(piece, c, w) for c, w in _col_blocks(D_MODEL, PROJ_BLOCK)]


def _retention_scores(p, rows, proj_ref, state_ref, head_ref, tail_ref, sdec_ref):
    q = proj_ref[rows, pl.ds(Q_R + p * PAIR, PAIR)]
    k = proj_ref[rows, pl.ds(K_R + p * PAIR, PAIR)]
    vb = proj_ref[rows, pl.ds(V_R + p * PAIR, PAIR)].astype(BF16)
    kt = k.T
    scores = _dot(q.astype(BF16), _block_diag_t(kt.astype(BF16)))
    state = state_ref[p]
    cross = _dot((q * head_ref[p]).astype(BF16), state.astype(BF16))
    kv = _dot((kt * tail_ref[p]).astype(BF16), vb)
    same_head = _left_half((PAIR, PAIR)) == (
        lax.broadcasted_iota(jnp.int32, (PAIR, PAIR), 0) < HEAD_DIM)
    state_ref[p] = state * sdec_ref[p] + jnp.where(same_head, kv, 0.0)
    return scores, cross, vb


def _retention_out(p, rows, proj_ref, out, ret_g):
    sq = out * out
    lh = _left_half(out.shape)
    ss_l = jnp.sum(jnp.where(lh, sq, 0.0), axis=-1, keepdims=True)
    ss_r = jnp.sum(jnp.where(lh, 0.0, sq), axis=-1, keepdims=True)
    ms = jnp.where(lh, ss_l, ss_r) * np.float32(1.0 / HEAD_DIM)
    ret = out * lax.rsqrt(ms + NORM_EPS) * ret_g[:, p * PAIR:(p + 1) * PAIR]
    g = proj_ref[rows, pl.ds(G_R + p * PAIR, PAIR)]
    return jax.nn.silu(g) * ret


def _softmax_pair(p, s, sink_ref, layer):
    es, inv = [], []
    for j in range(2):
        sj = s[:, j * 2 * CHUNK:(j + 1) * 2 * CHUNK]
        sink = sink_ref[layer, ATT_HEAD_ORDER[2 * p + j]]
        m = jnp.maximum(jnp.max(sj, axis=-1, keepdims=True), sink)
        e = jnp.exp(sj - m)
        es.append(e.astype(BF16))
        inv.append(1.0 / (jnp.sum(e, axis=-1, keepdims=True) + jnp.exp(sink - m)))
    return jnp.concatenate(es, axis=1), jnp.where(_left_half((CHUNK, PAIR)), inv[0], inv[1])


def _mixer_kernel(x_ref, xn_ref, g1_ref, w_in_ref, w_out_ref, ret_g_ref, ln_g_ref, ln_b_ref,
                  ws_ref, bs_ref, sink_ref, intra_ref, head_ref, tail_ref, sdec_ref, bias_ref,
                  o_ref, proj_ref, mixed_ref, wsb_ref, state_ref, kprev_ref, vprev_ref,
                  *, layer, tiles_per_seq):
    step = pl.program_id(0)
    g1 = g1_ref[...]
    chunk_rows = [pl.ds(c * CHUNK, CHUNK) for c in range(MIX_CHUNKS)]

    @pl.when(step == 0)
    def _():
        causal = lax.broadcasted_iota(jnp.int32, (CHUNK, CHUNK), 0) >= lax.broadcasted_iota(
            jnp.int32, (CHUNK, CHUNK), 1)
        for g in range(MLP_GROUPS):
            wsb_ref[:, pl.ds(g * CHUNK, CHUNK)] = jnp.where(causal, ws_ref[g], 0.0).astype(BF16)
        for job in _in_proj_jobs(lambda: x_ref[chunk_rows[0], :], g1, w_in_ref, proj_ref,
                                 chunk_rows[0]):
            job()

    seq_start = step % tiles_per_seq == 0

    @pl.when(seq_start)
    def _():
        state_ref[...] = jnp.zeros_like(state_ref)
        kprev_ref[...] = jnp.zeros_like(kprev_ref)
        vprev_ref[...] = jnp.zeros_like(vprev_ref)

    ret_g = ret_g_ref[...]
    ln_g = ln_g_ref[...]
    ln_b = ln_b_ref[...]
    first_block = seq_start.astype(jnp.int32)

    for c in range(MIX_CHUNKS):
        rows = chunk_rows[c]
        if c + 1 < MIX_CHUNKS:
            jobs = _in_proj_jobs(lambda c=c: x_ref[chunk_rows[c + 1], :], g1, w_in_ref, proj_ref,
                                 chunk_rows[c + 1])
        else:
            jobs = _in_proj_jobs(lambda: xn_ref[...], g1, w_in_ref, proj_ref, chunk_rows[0])
        if c > 0:
            jobs += _out_proj_jobs(x_ref, mixed_ref, w_out_ref, o_ref, chunk_rows[c - 1])
        jobs = iter(jobs)

        def fill():
            job = next(jobs, None)
            if job is not None:
                job()

        ret1 = [_retention_scores(p, rows, proj_ref, state_ref, head_ref, tail_ref, sdec_ref)
                for p in range(RET_PAIRS)]
        fill()

        kcur = (proj_ref[rows, pl.ds(K_A, PAIR)] * np.float32(HEAD_DIM ** -0.5)).T.astype(BF16)
        vcur = proj_ref[rows, pl.ds(V_A, PAIR)].astype(BF16)
        kbd = _block_diag_t(jnp.concatenate([kprev_ref[...], kcur], axis=1))
        vbd = _block_diag(jnp.concatenate([vprev_ref[...], vcur], axis=0))
        kprev_ref[...] = kcur
        vprev_ref[...] = vcur
        first = first_block if c == 0 else 0
        att_s = [_dot(proj_ref[rows, pl.ds(Q_A + p * PAIR, PAIR)].astype(BF16), kbd)
                 + bias_ref[first, p] for p in range(ATT_PAIRS)]
        z = _gelu(proj_ref[rows, pl.ds(Z_M, 2 * MLP_WIDTH)])
        u = z[:, :MLP_WIDTH]
        v = z[:, MLP_WIDTH:]
        mu = jnp.mean(v, axis=-1, keepdims=True)
        var = jnp.mean(jnp.square(v - mu), axis=-1, keepdims=True)
        vn = ((v - mu) * lax.rsqrt(var + NORM_EPS) * ln_g + ln_b).astype(BF16)
        fill()

        ret_out = [_dot((scores * intra_ref[p]).astype(BF16), _block_diag(vb)) + cross
                   for p, (scores, cross, vb) in enumerate(ret1)]
        fill()

        for p in range(MLP_PAIRS):
            cols = slice(p * PAIR, (p + 1) * PAIR)
            gate = _dot(wsb_ref[:, pl.ds(2 * p * CHUNK, 2 * CHUNK)], _block_diag(vn[:, cols]))
            mixed_ref[rows, pl.ds(MIX_MLP + p * PAIR, PAIR)] = (
                u[:, cols] * (gate + bs_ref[:, cols])).astype(BF16)
        att_p = [_softmax_pair(p, att_s[p], sink_ref, layer) for p in range(ATT_PAIRS)]
        fill()

        att_o = [_dot(e, vbd) for e, _ in att_p]
        fill()

        for p in range(RET_PAIRS):
            mixed_ref[rows, pl.ds(MIX_RET + p * PAIR, PAIR)] = _retention_out(
                p, rows, proj_ref, ret_out[p], ret_g).astype(BF16)
        fill()

        for p in range(ATT_PAIRS):
            mixed_ref[rows, pl.ds(MIX_ATT + p * PAIR, PAIR)] = (att_o[p] * att_p[p][1]).astype(BF16)
        fill()
        fill()
        assert next(jobs, None) is None

    for job in _out_proj_jobs(x_ref, mixed_ref, w_out_ref, o_ref, chunk_rows[MIX_CHUNKS - 1]):
        job()


def _ffn_kernel(x_ref, g2_ref, w_in_ref, w_out_ref, gf_ref, o_ref, act_ref, *, final_norm):
    x = x_ref[...]
    h = _rms_norm(x, g2_ref[...]).astype(BF16)
    for j in range(D_FF // FF_BLOCK):
        a = _dot(h, w_in_ref[:, pl.ds(j * FF_BLOCK, FF_BLOCK)])
        b = _dot(h, w_in_ref[:, pl.ds(D_FF + j * FF_BLOCK, FF_BLOCK)])
        act_ref[:, pl.ds(j * FF_BLOCK, FF_BLOCK)] = (jax.nn.silu(a) * b).astype(BF16)
    y = x + _dot(act_ref[...], w_out_ref[...])
    if final_norm:
        y = _rms_norm(y, gf_ref[...])
    o_ref[...] = y


def _layer_spec(stacked, layer):
    tail = (0,) * (stacked.ndim - 1)
    return pl.BlockSpec((None,) + stacked.shape[1:], lambda *_: (layer,) + tail,
                        pipeline_mode=pl.Buffered(1))


def _const_spec(arr):
    zeros = (0,) * arr.ndim
    return pl.BlockSpec(arr.shape, lambda *_: zeros, pipeline_mode=pl.Buffered(1))


def _mixer_call(x, layer, seq, g1, w_in, w_out, ret_g, ln_g, ln_b, ws, bs, sinks, tables):
    tokens = x.shape[0]
    n_tiles = tokens // MIX_TILE
    last_chunk = tokens // CHUNK - 1
    x_spec = pl.BlockSpec((MIX_TILE, D_MODEL), lambda s: (s, 0))
    next_spec = pl.BlockSpec(
        (CHUNK, D_MODEL), lambda s: (jnp.minimum((s + 1) * MIX_CHUNKS, last_chunk), 0))
    stacked = (g1, w_in, w_out, ret_g, ln_g, ln_b, ws, bs)
    return pl.pallas_call(
        functools.partial(_mixer_kernel, layer=layer, tiles_per_seq=seq // MIX_TILE),
        grid=(n_tiles,),
        in_specs=[x_spec, next_spec] + [_layer_spec(a, layer) for a in stacked]
        + [pl.BlockSpec(memory_space=pltpu.SMEM)] + [_const_spec(a) for a in tables],
        out_specs=x_spec,
        out_shape=jax.ShapeDtypeStruct(x.shape, x.dtype),
        scratch_shapes=[
            pltpu.VMEM((MIX_TILE, IN_WIDTH), F32),
            pltpu.VMEM((MIX_TILE, D_MODEL), BF16),
            pltpu.VMEM((CHUNK, MLP_GROUPS * CHUNK), BF16),
            pltpu.VMEM((RET_PAIRS, PAIR, PAIR), F32),
            pltpu.VMEM((CHUNK, PAIR), BF16),
            pltpu.VMEM((CHUNK, PAIR), BF16),
        ],
        compiler_params=pltpu.CompilerParams(
            dimension_semantics=("arbitrary",),
            vmem_limit_bytes=MIX_VMEM_BYTES),
        name="mixer",
    )(x, x, *stacked, sinks, *tables)


def _ffn_call(x, layer, g2, w_in, w_out, gf, final_norm):
    n_tiles = x.shape[0] // FFN_TILE
    x_spec = pl.BlockSpec((FFN_TILE, D_MODEL), lambda i: (i, 0))
    stacked = (g2, w_in, w_out)
    return pl.pallas_call(
        functools.partial(_ffn_kernel, final_norm=final_norm),
        grid=(n_tiles,),
        in_specs=[x_spec] + [_layer_spec(a, layer) for a in stacked] + [_const_spec(gf)],
        out_specs=x_spec,
        out_shape=jax.ShapeDtypeStruct(x.shape, x.dtype),
        scratch_shapes=[pltpu.VMEM((FFN_TILE, D_FF), BF16)],
        compiler_params=pltpu.CompilerParams(
            dimension_semantics=("arbitrary",),
            vmem_limit_bytes=FFN_VMEM_BYTES),
        name="ffn",
    )(x, *stacked, gf)


def _alibi_slopes(n):
    def pow2(m):
        start = 2.0 ** (-(2.0 ** -(math.log2(m) - 3)))
        return [start * start ** i for i in range(m)]
    if math.log2(n).is_integer():
        s = pow2(n)
    else:
        c = 2 ** int(math.floor(math.log2(n)))
        s = pow2(c) + pow2(2 * c)[0::2][: n - c]
    return np.array(s, dtype=np.float32)


def _tables():
    scale = np.float32(HEAD_DIM ** -0.5)
    log_g = jnp.log1p(-(2.0 ** (-RET_DECAY_BASE - jnp.arange(RET_HEADS, dtype=F32))))
    pos = jnp.arange(CHUNK, dtype=F32)
    diff = pos[:, None] - pos[None, :]
    intra = jnp.where(diff[None] >= 0,
                      jnp.exp(log_g[:, None, None] * jnp.maximum(diff, 0.0)[None]), 0.0) * scale
    intra_tab = intra.reshape(RET_PAIRS, 2, CHUNK, CHUNK).transpose(0, 2, 1, 3).reshape(
        RET_PAIRS, CHUNK, 2 * CHUNK)
    lane_gamma = jnp.repeat(log_g, HEAD_DIM).reshape(RET_PAIRS, 1, PAIR)
    head_tab = jnp.exp(lane_gamma * (pos + 1.0)[None, :, None])
    tail_tab = jnp.swapaxes(jnp.exp(lane_gamma * (CHUNK - 1.0 - pos)[None, :, None]) * scale, 1, 2)
    sdec_tab = jnp.broadcast_to(
        jnp.exp(lane_gamma * CHUNK).reshape(RET_PAIRS, PAIR, 1), (RET_PAIRS, PAIR, PAIR))

    qi = jnp.arange(CHUNK)
    kj = jnp.arange(2 * CHUNK)
    dist = CHUNK + qi[:, None] - kj[None, :]
    in_window = (dist >= 0) & (dist < WINDOW)
    allowed = jnp.stack([in_window, in_window & (kj[None, :] >= CHUNK)])
    slopes = jnp.asarray(_alibi_slopes(ATT_HEADS))[jnp.asarray(ATT_HEAD_ORDER)]
    bias = jnp.where(allowed[:, None], -(slopes[None, :, None, None] * dist.astype(F32)),
                     -jnp.inf)
    bias_tab = bias.reshape(2, ATT_PAIRS, 2, CHUNK, 2 * CHUNK).transpose(0, 1, 3, 2, 4).reshape(
        2, ATT_PAIRS, CHUNK, 4 * CHUNK)
    return intra_tab, head_tab, tail_tab, sdec_tab, bias_tab


def kernel(x, norm1_g, w_in, ret_norm_g, mlp_ln_g, mlp_ln_b, w_spatial, b_spatial, attn_sinks,
           w_out, norm2_g, w_ffn_in, w_ffn_out, final_norm_g):
    batch, seq, d_model = x.shape
    depth = w_in.shape[0]
    tokens = batch * seq
    assert d_model == D_MODEL and seq % MIX_TILE == 0 and tokens % FFN_TILE == 0

    def pair_heads(w, axis):
        shape = w.shape
        w = w.reshape(shape[:axis] + (ATT_KV_HEADS, ATT_GROUP, HEAD_DIM) + shape[axis + 1:])
        return jnp.swapaxes(w, axis, axis + 1).reshape(shape)

    w_in_b = w_in.astype(BF16).at[:, :, Q_A:K_A].set(
        pair_heads(w_in[:, :, Q_A:K_A], 2).astype(BF16))
    w_out_b = w_out.astype(BF16).at[:, MIX_ATT:, :].set(
        pair_heads(w_out[:, MIX_ATT:, :], 1).astype(BF16))
    w_ffn_in_b = w_ffn_in.astype(BF16)
    w_ffn_out_b = w_ffn_out.astype(BF16)
    bs_tab = jnp.repeat(jnp.swapaxes(b_spatial, 1, 2), HEAD_DIM, axis=2)
    tables = _tables()
    rows = lambda a: a.reshape(depth, 1, -1)

    x = x.reshape(tokens, d_model)
    for l in range(depth):
        x = _mixer_call(x, l, seq, rows(norm1_g), w_in_b, w_out_b, rows(ret_norm_g),
                        rows(mlp_ln_g), rows(mlp_ln_b), w_spatial, bs_tab, attn_sinks, tables)
        x = _ffn_call(x, l, rows(norm2_g), w_ffn_in_b, w_ffn_out_b, final_norm_g.reshape(1, -1),
                      final_norm=(l == depth - 1))
    return x.reshape(batch, seq, d_model)
```

```python
import functools
import math

import jax
import jax.numpy as jnp
import numpy as np
from jax import lax
from jax.experimental import pallas as pl
from jax.experimental.pallas import tpu as pltpu

D_MODEL = 1024
HEAD_DIM = 64
CHUNK = 128
RET_HEADS = 6
RET_WIDTH = RET_HEADS * HEAD_DIM
MLP_GROUPS = 4
MLP_WIDTH = MLP_GROUPS * HEAD_DIM
ATT_HEADS = 6
ATT_KV_HEADS = 2
ATT_WIDTH = ATT_HEADS * HEAD_DIM
ATT_KV_WIDTH = ATT_KV_HEADS * HEAD_DIM
WINDOW = 128
IN_WIDTH = 4 * RET_WIDTH + 2 * MLP_WIDTH + ATT_WIDTH + 2 * ATT_KV_WIDTH
D_FF = -(-8 * D_MODEL // (3 * 256)) * 256
RET_DECAY_BASE = 5.0
NORM_EPS = 1e-6

LANES = 128
BF16_SUBLANES = 16
PAIR = 2 * HEAD_DIM
assert PAIR == LANES and CHUNK == LANES and WINDOW == CHUNK

Q_R, K_R, V_R, G_R = 0, RET_WIDTH, 2 * RET_WIDTH, 3 * RET_WIDTH
Z_M = 4 * RET_WIDTH
Q_A = Z_M + 2 * MLP_WIDTH
K_A = Q_A + ATT_WIDTH
V_A = K_A + ATT_KV_WIDTH
MIX_RET, MIX_MLP, MIX_ATT = 0, RET_WIDTH, RET_WIDTH + MLP_WIDTH

RET_PAIRS = RET_HEADS // 2
MLP_PAIRS = MLP_GROUPS // 2
ATT_PAIRS = ATT_HEADS // 2
ATT_GROUP = ATT_HEADS // ATT_KV_HEADS
ATT_HEAD_ORDER = tuple(h for p in range(ATT_PAIRS) for h in (p, p + ATT_GROUP))

MIX_TILE = 1024
MIX_CHUNKS = MIX_TILE // CHUNK
PROJ_BLOCK = 512
FFN_TILE = 1024
FF_BLOCK = 256
MIX_VMEM_BYTES = 52 * 1024 * 1024
FFN_VMEM_BYTES = 56 * 1024 * 1024

BF16 = jnp.bfloat16
F32 = jnp.float32


def _dot(a, b):
    return jnp.dot(a, b, preferred_element_type=F32)


def _left_half(shape):
    return lax.broadcasted_iota(jnp.int32, shape, len(shape) - 1) < HEAD_DIM


def _block_diag(pair):
    left = _left_half(pair.shape)
    zero = jnp.zeros_like(pair)
    return jnp.concatenate([jnp.where(left, pair, zero), jnp.where(left, zero, pair)], axis=0)


def _block_diag_t(pair_t):
    top = lax.broadcasted_iota(jnp.int32, pair_t.shape, 0) < HEAD_DIM
    zero = jnp.zeros_like(pair_t)
    return jnp.concatenate([jnp.where(top, pair_t, zero), jnp.where(top, zero, pair_t)], axis=1)


def _rms_norm(x, g):
    ms = jnp.mean(x * x, axis=-1, keepdims=True)
    return x * lax.rsqrt(ms + NORM_EPS) * g


def _gelu(x):
    return 0.5 * x * (1.0 + lax.erf(x * np.float32(math.sqrt(0.5))))


def _col_blocks(width, block):
    return [(c, min(block, width - c)) for c in range(0, width, block)]


def _in_proj_jobs(x_rows, g1, w_in_ref, proj_ref, rows):
    h = []

    def piece(col, width):
        if not h:
            h.append(_rms_norm(x_rows(), g1).astype(BF16))
        proj_ref[rows, pl.ds(col, width)] = _dot(h[0], w_in_ref[:, pl.ds(col, width)])

    return [functools.partial(piece, c, w) for c, w in _col_blocks(IN_WIDTH, PROJ_BLOCK)]


def _out_proj_jobs(x_ref, mixed_ref, w_out_ref, o_ref, rows):
    def piece(col, width):
        cols = pl.ds(col, width)
        o_ref[rows, cols] = x_ref[rows, cols] + _dot(mixed_ref[rows, :], w_out_ref[:, cols])

    return [functools.partial(piece, c, w) for c, w in _col_blocks(D_MODEL, PROJ_BLOCK)]


def _retention_scores(p, rows, proj_ref, state_ref, head_ref, tail_ref, sdec_ref):
    q = proj_ref[rows, pl.ds(Q_R + p * PAIR, PAIR)]
    k = proj_ref[rows, pl.ds(K_R + p * PAIR, PAIR)]
    vb = proj_ref[rows, pl.ds(V_R + p * PAIR, PAIR)].astype(BF16)
    kt = k.T
    scores = _dot(q.astype(BF16), _block_diag_t(kt.astype(BF16)))
    state = state_ref[p]
    cross = _dot((q * head_ref[p]).astype(BF16), state.astype(BF16))
    kv = _dot((kt * tail_ref[p]).astype(BF16), vb)
    same_head = _left_half((PAIR, PAIR)) == (
        lax.broadcasted_iota(jnp.int32, (PAIR, PAIR), 0) < HEAD_DIM)
    state_ref[p] = state * sdec_ref[p] + jnp.where(same_head, kv, 0.0)
    return scores, cross, vb


def _retention_out(p, rows, proj_ref, out, ret_g):
    sq = out * out
    lh = _left_half(out.shape)
    ss_l = jnp.sum(jnp.where(lh, sq, 0.0), axis=-1, keepdims=True)
    ss_r = jnp.sum(jnp.where(lh, 0.0, sq), axis=-1, keepdims=True)
    ms = jnp.where(lh, ss_l, ss_r) * np.float32(1.0 / HEAD_DIM)
    ret = out * lax.rsqrt(ms + NORM_EPS) * ret_g[:, p * PAIR:(p + 1) * PAIR]
    g = proj_ref[rows, pl.ds(G_R + p * PAIR, PAIR)]
    return jax.nn.silu(g) * ret


def _softmax_pair(p, s, sink_ref, layer):
    es, inv = [], []
    for j in range(2):
        sj = s[:, j * 2 * CHUNK:(j + 1) * 2 * CHUNK]
        sink = sink_ref[layer, ATT_HEAD_ORDER[2 * p + j]]
        m = jnp.maximum(jnp.max(sj, axis=-1, keepdims=True), sink)
        e = jnp.exp(sj - m)
        es.append(e.astype(BF16))
        inv.append(1.0 / (jnp.sum(e, axis=-1, keepdims=True) + jnp.exp(sink - m)))
    return jnp.concatenate(es, axis=1), jnp.where(_left_half((CHUNK, PAIR)), inv[0], inv[1])


def _mixer_kernel(x_ref, xn_ref, g1_ref, w_in_ref, w_out_ref, ret_g_ref, ln_g_ref, ln_b_ref,
                  ws_ref, bs_ref, sink_ref, intra_ref, head_ref, tail_ref, sdec_ref, bias_ref,
                  o_ref, proj_ref, mixed_ref, wsb_ref, state_ref, kprev_ref, vprev_ref,
                  *, layer, tiles_per_seq):
    step = pl.program_id(0)
    g1 = g1_ref[...]
    chunk_rows = [pl.ds(c * CHUNK, CHUNK) for c in range(MIX_CHUNKS)]

    @pl.when(step == 0)
    def _():
        causal = lax.broadcasted_iota(jnp.int32, (CHUNK, CHUNK), 0) >= lax.broadcasted_iota(
            jnp.int32, (CHUNK, CHUNK), 1)
        for g in range(MLP_GROUPS):
            wsb_ref[:, pl.ds(g * CHUNK, CHUNK)] = jnp.where(causal, ws_ref[g], 0.0).astype(BF16)
        for job in _in_proj_jobs(lambda: x_ref[chunk_rows[0], :], g1, w_in_ref, proj_ref,
                                 chunk_rows[0]):
            job()

    seq_start = step % tiles_per_seq == 0

    @pl.when(seq_start)
    def _():
        state_ref[...] = jnp.zeros_like(state_ref)
        kprev_ref[...] = jnp.zeros_like(kprev_ref)
        vprev_ref[...] = jnp.zeros_like(vprev_ref)

    ret_g = ret_g_ref[...]
    ln_g = ln_g_ref[...]
    ln_b = ln_b_ref[...]
    first_block = seq_start.astype(jnp.int32)

    for c in range(MIX_CHUNKS):
        rows = chunk_rows[c]
        if c + 1 < MIX_CHUNKS:
            jobs = _in_proj_jobs(lambda c=c: x_ref[chunk_rows[c + 1], :], g1, w_in_ref, proj_ref,
                                 chunk_rows[c + 1])
        else:
            jobs = _in_proj_jobs(lambda: xn_ref[...], g1, w_in_ref, proj_ref, chunk_rows[0])
        if c > 0:
            jobs += _out_proj_jobs(x_ref, mixed_ref, w_out_ref, o_ref, chunk_rows[c - 1])
        jobs = iter(jobs)

        def fill():
            job = next(jobs, None)
            if job is not None:
                job()

        ret1 = [_retention_scores(p, rows, proj_ref, state_ref, head_ref, tail_ref, sdec_ref)
                for p in range(RET_PAIRS)]
        fill()

        kcur = (proj_ref[rows, pl.ds(K_A, PAIR)] * np.float32(HEAD_DIM ** -0.5)).T.astype(BF16)
        vcur = proj_ref[rows, pl.ds(V_A, PAIR)].astype(BF16)
        kbd = _block_diag_t(jnp.concatenate([kprev_ref[...], kcur], axis=1))
        vbd = _block_diag(jnp.concatenate([vprev_ref[...], vcur], axis=0))
        kprev_ref[...] = kcur
        vprev_ref[...] = vcur
        first = first_block if c == 0 else 0
        att_s = [_dot(proj_ref[rows, pl.ds(Q_A + p * PAIR, PAIR)].astype(BF16), kbd)
                 + bias_ref[first, p] for p in range(ATT_PAIRS)]
        z = _gelu(proj_ref[rows, pl.ds(Z_M, 2 * MLP_WIDTH)])
        u = z[:, :MLP_WIDTH]
        v = z[:, MLP_WIDTH:]
        mu = jnp.mean(v, axis=-1, keepdims=True)
        var = jnp.mean(jnp.square(v - mu), axis=-1, keepdims=True)
        vn = ((v - mu) * lax.rsqrt(var + NORM_EPS) * ln_g + ln_b).astype(BF16)
        fill()

        ret_out = [_dot((scores * intra_ref[p]).astype(BF16), _block_diag(vb)) + cross
                   for p, (scores, cross, vb) in enumerate(ret1)]
        fill()

        for p in range(MLP_PAIRS):
            cols = slice(p * PAIR, (p + 1) * PAIR)
            gate = _dot(wsb_ref[:, pl.ds(2 * p * CHUNK, 2 * CHUNK)], _block_diag(vn[:, cols]))
            mixed_ref[rows, pl.ds(MIX_MLP + p * PAIR, PAIR)] = (
                u[:, cols] * (gate + bs_ref[:, cols])).astype(BF16)
        att_p = [_softmax_pair(p, att_s[p], sink_ref, layer) for p in range(ATT_PAIRS)]
        fill()

        att_o = [_dot(e, vbd) for e, _ in att_p]
        fill()

        for p in range(RET_PAIRS):
            mixed_ref[rows, pl.ds(MIX_RET + p * PAIR, PAIR)] = _retention_out(
                p, rows, proj_ref, ret_out[p], ret_g).astype(BF16)
        fill()

        for p in range(ATT_PAIRS):
            mixed_ref[rows, pl.ds(MIX_ATT + p * PAIR, PAIR)] = (att_o[p] * att_p[p][1]).astype(BF16)
        fill()
        fill()
        assert next(jobs, None) is None

    for job in _out_proj_jobs(x_ref, mixed_ref, w_out_ref, o_ref, chunk_rows[MIX_CHUNKS - 1]):
        job()


def _ffn_kernel(*refs, final_norm, n_cast):
    x_ref, g2_ref, w_in_ref, w_out_ref, gf_ref = refs[:5]
    cast_src = refs[5:5 + n_cast]
    o_ref = refs[5 + n_cast]
    cast_dst = refs[6 + n_cast:6 + 2 * n_cast]
    act_ref = refs[6 + 2 * n_cast]

    x = x_ref[...]
    h = _rms_norm(x, g2_ref[...]).astype(BF16)
    for j in range(D_FF // FF_BLOCK):
        a = _dot(h, w_in_ref[:, pl.ds(j * FF_BLOCK, FF_BLOCK)])
        b = _dot(h, w_in_ref[:, pl.ds(D_FF + j * FF_BLOCK, FF_BLOCK)])
        act_ref[:, pl.ds(j * FF_BLOCK, FF_BLOCK)] = (jax.nn.silu(a) * b).astype(BF16)
        if j < n_cast:
            cast_dst[j][...] = cast_src[j][...].astype(BF16)
    y = x + _dot(act_ref[...], w_out_ref[...])
    if final_norm:
        y = _rms_norm(y, gf_ref[...])
    o_ref[...] = y


def _layer_spec(stacked, layer):
    tail = (0,) * (stacked.ndim - 1)
    return pl.BlockSpec((None,) + stacked.shape[1:], lambda *_: (layer,) + tail,
                        pipeline_mode=pl.Buffered(1))


def _const_spec(arr):
    zeros = (0,) * arr.ndim
    return pl.BlockSpec(arr.shape, lambda *_: zeros, pipeline_mode=pl.Buffered(1))


def _mixer_call(x, layer, seq, g1, w_in, w_out, ret_g, ln_g, ln_b, ws, bs, sinks, tables):
    tokens = x.shape[0]
    n_tiles = tokens // MIX_TILE
    last_chunk = tokens // CHUNK - 1
    x_spec = pl.BlockSpec((MIX_TILE, D_MODEL), lambda s: (s, 0))
    next_spec = pl.BlockSpec(
        (CHUNK, D_MODEL), lambda s: (jnp.minimum((s + 1) * MIX_CHUNKS, last_chunk), 0))
    stacked = (ret_g, ln_g, ln_b, ws, bs)
    return pl.pallas_call(
        functools.partial(_mixer_kernel, layer=layer, tiles_per_seq=seq // MIX_TILE),
        grid=(n_tiles,),
        in_specs=[x_spec, next_spec, _layer_spec(g1, layer), _const_spec(w_in), _const_spec(w_out)]
        + [_layer_spec(a, layer) for a in stacked]
        + [pl.BlockSpec(memory_space=pltpu.SMEM)] + [_const_spec(a) for a in tables],
        out_specs=x_spec,
        out_shape=jax.ShapeDtypeStruct(x.shape, x.dtype),
        scratch_shapes=[
            pltpu.VMEM((MIX_TILE, IN_WIDTH), F32),
            pltpu.VMEM((MIX_TILE, D_MODEL), BF16),
            pltpu.VMEM((CHUNK, MLP_GROUPS * CHUNK), BF16),
            pltpu.VMEM((RET_PAIRS, PAIR, PAIR), F32),
            pltpu.VMEM((CHUNK, PAIR), BF16),
            pltpu.VMEM((CHUNK, PAIR), BF16),
        ],
        compiler_params=pltpu.CompilerParams(
            dimension_semantics=("arbitrary",),
            vmem_limit_bytes=MIX_VMEM_BYTES),
        name="mixer",
    )(x, x, g1, w_in, w_out, *stacked, sinks, *tables)


def _ffn_call(x, layer, g2, w_in, w_out, gf, final_norm, next_weights):
    n_tiles = x.shape[0] // FFN_TILE
    x_spec = pl.BlockSpec((FFN_TILE, D_MODEL), lambda i: (i, 0))
    cast_in, cast_out, cast_shapes = [], [], []
    for w in next_weights:
        rows, cols = w.shape[1] // n_tiles, w.shape[2]
        assert rows * n_tiles == w.shape[1] and rows % BF16_SUBLANES == 0
        cast_in.append(pl.BlockSpec((None, rows, cols), lambda i: (layer + 1, i, 0)))
        cast_out.append(pl.BlockSpec((rows, cols), lambda i: (i, 0)))
        cast_shapes.append(jax.ShapeDtypeStruct(w.shape[1:], BF16))
    out = pl.pallas_call(
        functools.partial(_ffn_kernel, final_norm=final_norm, n_cast=len(next_weights)),
        grid=(n_tiles,),
        in_specs=[x_spec, _layer_spec(g2, layer), _const_spec(w_in), _const_spec(w_out),
                  _const_spec(gf)] + cast_in,
        out_specs=[x_spec] + cast_out,
        out_shape=[jax.ShapeDtypeStruct(x.shape, x.dtype)] + cast_shapes,
        scratch_shapes=[pltpu.VMEM((FFN_TILE, D_FF), BF16)],
        compiler_params=pltpu.CompilerParams(
            dimension_semantics=("arbitrary",),
            vmem_limit_bytes=FFN_VMEM_BYTES),
        name="ffn",
    )(x, g2, w_in, w_out, gf, *next_weights)
    return out[0], out[1:]


def _alibi_slopes(n):
    def pow2(m):
        start = 2.0 ** (-(2.0 ** -(math.log2(m) - 3)))
        return [start * start ** i for i in range(m)]
    if math.log2(n).is_integer():
        s = pow2(n)
    else:
        c = 2 ** int(math.floor(math.log2(n)))
        s = pow2(c) + pow2(2 * c)[0::2][: n - c]
    return np.array(s, dtype=np.float32)


def _tables():
    scale = np.float32(HEAD_DIM ** -0.5)
    log_g = jnp.log1p(-(2.0 ** (-RET_DECAY_BASE - jnp.arange(RET_HEADS, dtype=F32))))
    pos = jnp.arange(CHUNK, dtype=F32)
    diff = pos[:, None] - pos[None, :]
    intra = jnp.where(diff[None] >= 0,
                      jnp.exp(log_g[:, None, None] * jnp.maximum(diff, 0.0)[None]), 0.0) * scale
    intra_tab = intra.reshape(RET_PAIRS, 2, CHUNK, CHUNK).transpose(0, 2, 1, 3).reshape(
        RET_PAIRS, CHUNK, 2 * CHUNK)
    lane_gamma = jnp.repeat(log_g, HEAD_DIM).reshape(RET_PAIRS, 1, PAIR)
    head_tab = jnp.exp(lane_gamma * (pos + 1.0)[None, :, None])
    tail_tab = jnp.swapaxes(jnp.exp(lane_gamma * (CHUNK - 1.0 - pos)[None, :, None]) * scale, 1, 2)
    sdec_tab = jnp.broadcast_to(
        jnp.exp(lane_gamma * CHUNK).reshape(RET_PAIRS, PAIR, 1), (RET_PAIRS, PAIR, PAIR))

    qi = jnp.arange(CHUNK)
    kj = jnp.arange(2 * CHUNK)
    dist = CHUNK + qi[:, None] - kj[None, :]
    in_window = (dist >= 0) & (dist < WINDOW)
    allowed = jnp.stack([in_window, in_window & (kj[None, :] >= CHUNK)])
    slopes = jnp.asarray(_alibi_slopes(ATT_HEADS))[jnp.asarray(ATT_HEAD_ORDER)]
    bias = jnp.where(allowed[:, None], -(slopes[None, :, None, None] * dist.astype(F32)),
                     -jnp.inf)
    bias_tab = bias.reshape(2, ATT_PAIRS, 2, CHUNK, 2 * CHUNK).transpose(0, 1, 3, 2, 4).reshape(
        2, ATT_PAIRS, CHUNK, 4 * CHUNK)
    return intra_tab, head_tab, tail_tab, sdec_tab, bias_tab


def kernel(x, norm1_g, w_in, ret_norm_g, mlp_ln_g, mlp_ln_b, w_spatial, b_spatial, attn_sinks,
           w_out, norm2_g, w_ffn_in, w_ffn_out, final_norm_g):
    batch, seq, d_model = x.shape
    depth = w_in.shape[0]
    tokens = batch * seq
    assert d_model == D_MODEL and seq % MIX_TILE == 0 and tokens % FFN_TILE == 0

    def pair_heads(w, axis):
        shape = w.shape
        w = w.reshape(shape[:axis] + (ATT_KV_HEADS, ATT_GROUP, HEAD_DIM) + shape[axis + 1:])
        return jnp.swapaxes(w, axis, axis + 1).reshape(shape)

    qa_cols = pair_heads(w_in[:, :, Q_A:K_A], 2).astype(BF16)
    att_rows = pair_heads(w_out[:, MIX_ATT:, :], 1).astype(BF16)
    f32_weights = (w_in, w_out, w_ffn_in, w_ffn_out)
    weights = [w[0].astype(BF16) for w in f32_weights]
    bs_tab = jnp.repeat(jnp.swapaxes(b_spatial, 1, 2), HEAD_DIM, axis=2)
    tables = _tables()
    rows = lambda a: a.reshape(depth, 1, -1)

    x = x.reshape(tokens, d_model)
    for l in range(depth):
        w_in_b = weights[0].at[:, Q_A:K_A].set(qa_cols[l])
        w_out_b = weights[1].at[MIX_ATT:, :].set(att_rows[l])
        x = _mixer_call(x, l, seq, rows(norm1_g), w_in_b, w_out_b, rows(ret_norm_g),
                        rows(mlp_ln_g), rows(mlp_ln_b), w_spatial, bs_tab, attn_sinks, tables)
        last = l == depth - 1
        x, next_weights = _ffn_call(x, l, rows(norm2_g), weights[2], weights[3],
                                    final_norm_g.reshape(1, -1), final_norm=last,
                                    next_weights=() if last else f32_weights)
        weights = next_weights
    return x.reshape(batch, seq, d_model)
```

```python
import functools
import math

import jax
import jax.numpy as jnp
import numpy as np
from jax import lax
from jax.experimental import pallas as pl
from jax.experimental.pallas import tpu as pltpu

D_MODEL = 1024
HEAD_DIM = 64
CHUNK = 128
RET_HEADS = 6
RET_WIDTH = RET_HEADS * HEAD_DIM
MLP_GROUPS = 4
MLP_WIDTH = MLP_GROUPS * HEAD_DIM
ATT_HEADS = 6
ATT_KV_HEADS = 2
ATT_WIDTH = ATT_HEADS * HEAD_DIM
ATT_KV_WIDTH = ATT_KV_HEADS * HEAD_DIM
WINDOW = 128
IN_WIDTH = 4 * RET_WIDTH + 2 * MLP_WIDTH + ATT_WIDTH + 2 * ATT_KV_WIDTH
D_FF = -(-8 * D_MODEL // (3 * 256)) * 256
RET_DECAY_BASE = 5.0
NORM_EPS = 1e-6

LANES = 128
BF16_SUBLANES = 16
PAIR = 2 * HEAD_DIM
assert PAIR == LANES and CHUNK == LANES and WINDOW == CHUNK

Q_R, K_R, V_R, G_R = 0, RET_WIDTH, 2 * RET_WIDTH, 3 * RET_WIDTH
Z_M = 4 * RET_WIDTH
Q_A = Z_M + 2 * MLP_WIDTH
K_A = Q_A + ATT_WIDTH
V_A = K_A + ATT_KV_WIDTH
MIX_RET, MIX_MLP, MIX_ATT = 0, RET_WIDTH, RET_WIDTH + MLP_WIDTH

RET_PAIRS = RET_HEADS // 2
MLP_PAIRS = MLP_GROUPS // 2
ATT_PAIRS = ATT_HEADS // 2
ATT_GROUP = ATT_HEADS // ATT_KV_HEADS
ATT_HEAD_ORDER = tuple(h for p in range(ATT_PAIRS) for h in (p, p + ATT_GROUP))

MIX_TILE = 1024
MIX_CHUNKS = MIX_TILE // CHUNK
PROJ_BLOCK = 512
FFN_TILE = 1024
FFN_SUB = 512
FF_BLOCK = 256
MIX_VMEM_BYTES = 52 * 1024 * 1024
FFN_VMEM_BYTES = 56 * 1024 * 1024

BF16 = jnp.bfloat16
F32 = jnp.float32


def _dot(a, b):
    return jnp.dot(a, b, preferred_element_type=F32)


def _left_half(shape):
    return lax.broadcasted_iota(jnp.int32, shape, len(shape) - 1) < HEAD_DIM


def _block_diag(pair):
    left = _left_half(pair.shape)
    zero = jnp.zeros_like(pair)
    return jnp.concatenate([jnp.where(left, pair, zero), jnp.where(left, zero, pair)], axis=0)


def _block_diag_t(pair_t):
    top = lax.broadcasted_iota(jnp.int32, pair_t.shape, 0) < HEAD_DIM
    zero = jnp.zeros_like(pair_t)
    return jnp.concatenate([jnp.where(top, pair_t, zero), jnp.where(top, zero, pair_t)], axis=1)


def _rms_norm(x, g):
    ms = jnp.mean(x * x, axis=-1, keepdims=True)
    return x * lax.rsqrt(ms + NORM_EPS) * g


def _gelu(x):
    return 0.5 * x * (1.0 + lax.erf(x * np.float32(math.sqrt(0.5))))


def _col_blocks(width, block):
    return [(c, min(block, width - c)) for c in range(0, width, block)]


def _in_proj_jobs(x_rows, g1, w_in_ref, w_qa_ref, proj_ref, rows):
    h = []

    def piece(w_ref, src, dst, width):
        if not h:
            h.append(_rms_norm(x_rows(), g1).astype(BF16))
        proj_ref[rows, pl.ds(dst, width)] = _dot(h[0], w_ref[:, pl.ds(src, width)])

    blocks = [(w_in_ref, c, c, w) for c, w in _col_blocks(Q_A, PROJ_BLOCK)]
    blocks += [(w_qa_ref, 0, Q_A, ATT_WIDTH), (w_in_ref, K_A, K_A, IN_WIDTH - K_A)]
    return [functools.partial(piece, *b) for b in blocks]


def _out_proj_jobs(x_ref, mixed_ref, w_out_ref, o_ref, rows):
    def piece(col, width):
        cols = pl.ds(col, width)
        o_ref[rows, cols] = x_ref[rows, cols] + _dot(mixed_ref[rows, :], w_out_ref[:, cols])

    return [functools.partial(piece, c, w) for c, w in _col_blocks(D_MODEL, PROJ_BLOCK)]


def _retention_scores(p, rows, proj_ref, state_ref, head_ref, tail_ref, sdec_ref):
    q = proj_ref[rows, pl.ds(Q_R + p * PAIR, PAIR)]
    k = proj_ref[rows, pl.ds(K_R + p * PAIR, PAIR)]
    vb = proj_ref[rows, pl.ds(V_R + p * PAIR, PAIR)].astype(BF16)
    kt = k.T
    scores = _dot(q.astype(BF16), _block_diag_t(kt.astype(BF16)))
    state = state_ref[p]
    cross = _dot((q * head_ref[p]).astype(BF16), state.astype(BF16))
    kv = _dot((kt * tail_ref[p]).astype(BF16), vb)
    same_head = _left_half((PAIR, PAIR)) == (
        lax.broadcasted_iota(jnp.int32, (PAIR, PAIR), 0) < HEAD_DIM)
    state_ref[p] = state * sdec_ref[p] + jnp.where(same_head, kv, 0.0)
    return scores, cross, vb


def _retention_out(p, rows, proj_ref, out, ret_g):
    sq = out * out
    lh = _left_half(out.shape)
    ss_l = jnp.sum(jnp.where(lh, sq, 0.0), axis=-1, keepdims=True)
    ss_r = jnp.sum(jnp.where(lh, 0.0, sq), axis=-1, keepdims=True)
    ms = jnp.where(lh, ss_l, ss_r) * np.float32(1.0 / HEAD_DIM)
    ret = out * lax.rsqrt(ms + NORM_EPS) * ret_g[:, p * PAIR:(p + 1) * PAIR]
    g = proj_ref[rows, pl.ds(G_R + p * PAIR, PAIR)]
    return jax.nn.silu(g) * ret


def _softmax_pair(p, s, sink_ref, layer):
    es, inv = [], []
    for j in range(2):
        sj = s[:, j * 2 * CHUNK:(j + 1) * 2 * CHUNK]
        sink = sink_ref[layer, ATT_HEAD_ORDER[2 * p + j]]
        m = jnp.maximum(jnp.max(sj, axis=-1, keepdims=True), sink)
        e = jnp.exp(sj - m)
        es.append(e.astype(BF16))
        inv.append(1.0 / (jnp.sum(e, axis=-1, keepdims=True) + jnp.exp(sink - m)))
    return jnp.concatenate(es, axis=1), jnp.where(_left_half((CHUNK, PAIR)), inv[0], inv[1])


def _cast_blocks(cast_src, cast_dst):
    for src, dst in zip(cast_src, cast_dst):
        dst[...] = src[...].astype(BF16)


def _mixer_kernel(*refs, layer, tiles_per_seq, n_cast):
    (x_ref, xn_ref, g1_ref, w_in_ref, w_qa_ref, w_out_ref, ret_g_ref, ln_g_ref, ln_b_ref, ws_ref,
     bs_ref, sink_ref, intra_ref, head_ref, tail_ref, sdec_ref, bias_ref) = refs[:17]
    cast_src = refs[17:17 + n_cast]
    o_ref = refs[17 + n_cast]
    cast_dst = refs[18 + n_cast:18 + 2 * n_cast]
    proj_ref, mixed_ref, wsb_ref, state_ref, kprev_ref, vprev_ref = refs[18 + 2 * n_cast:]
    step = pl.program_id(0)
    g1 = g1_ref[...]
    chunk_rows = [pl.ds(c * CHUNK, CHUNK) for c in range(MIX_CHUNKS)]

    @pl.when(step == 0)
    def _():
        causal = lax.broadcasted_iota(jnp.int32, (CHUNK, CHUNK), 0) >= lax.broadcasted_iota(
            jnp.int32, (CHUNK, CHUNK), 1)
        for g in range(MLP_GROUPS):
            wsb_ref[:, pl.ds(g * CHUNK, CHUNK)] = jnp.where(causal, ws_ref[g], 0.0).astype(BF16)
        for job in _in_proj_jobs(lambda: x_ref[chunk_rows[0], :], g1, w_in_ref, w_qa_ref,
                                 proj_ref, chunk_rows[0]):
            job()

    seq_start = step % tiles_per_seq == 0

    @pl.when(seq_start)
    def _():
        state_ref[...] = jnp.zeros_like(state_ref)
        kprev_ref[...] = jnp.zeros_like(kprev_ref)
        vprev_ref[...] = jnp.zeros_like(vprev_ref)

    ret_g = ret_g_ref[...]
    ln_g = ln_g_ref[...]
    ln_b = ln_b_ref[...]
    first_block = seq_start.astype(jnp.int32)

    for c in range(MIX_CHUNKS):
        rows = chunk_rows[c]
        if c + 1 < MIX_CHUNKS:
            jobs = _in_proj_jobs(lambda c=c: x_ref[chunk_rows[c + 1], :], g1, w_in_ref, w_qa_ref,
                                 proj_ref, chunk_rows[c + 1])
        else:
            jobs = _in_proj_jobs(lambda: xn_ref[...], g1, w_in_ref, w_qa_ref, proj_ref,
                                 chunk_rows[0])
        if c > 0:
            jobs += _out_proj_jobs(x_ref, mixed_ref, w_out_ref, o_ref, chunk_rows[c - 1])
        jobs = iter(jobs)

        def fill():
            job = next(jobs, None)
            if job is not None:
                job()

        ret1 = [_retention_scores(p, rows, proj_ref, state_ref, head_ref, tail_ref, sdec_ref)
                for p in range(RET_PAIRS)]
        fill()

        kcur = (proj_ref[rows, pl.ds(K_A, PAIR)] * np.float32(HEAD_DIM ** -0.5)).T.astype(BF16)
        vcur = proj_ref[rows, pl.ds(V_A, PAIR)].astype(BF16)
        kbd = _block_diag_t(jnp.concatenate([kprev_ref[...], kcur], axis=1))
        vbd = _block_diag(jnp.concatenate([vprev_ref[...], vcur], axis=0))
        kprev_ref[...] = kcur
        vprev_ref[...] = vcur
        first = first_block if c == 0 else 0
        att_s = [_dot(proj_ref[rows, pl.ds(Q_A + p * PAIR, PAIR)].astype(BF16), kbd)
                 + bias_ref[first, p] for p in range(ATT_PAIRS)]
        z = _gelu(proj_ref[rows, pl.ds(Z_M, 2 * MLP_WIDTH)])
        u = z[:, :MLP_WIDTH]
        v = z[:, MLP_WIDTH:]
        mu = jnp.mean(v, axis=-1, keepdims=True)
        var = jnp.mean(jnp.square(v - mu), axis=-1, keepdims=True)
        vn = ((v - mu) * lax.rsqrt(var + NORM_EPS) * ln_g + ln_b).astype(BF16)
        fill()

        ret_out = [_dot((scores * intra_ref[p]).astype(BF16), _block_diag(vb)) + cross
                   for p, (scores, cross, vb) in enumerate(ret1)]
        fill()

        for p in range(MLP_PAIRS):
            cols = slice(p * PAIR, (p + 1) * PAIR)
            gate = _dot(wsb_ref[:, pl.ds(2 * p * CHUNK, 2 * CHUNK)], _block_diag(vn[:, cols]))
            mixed_ref[rows, pl.ds(MIX_MLP + p * PAIR, PAIR)] = (
                u[:, cols] * (gate + bs_ref[:, cols])).astype(BF16)
        att_p = [_softmax_pair(p, att_s[p], sink_ref, layer) for p in range(ATT_PAIRS)]
        fill()

        att_o = [_dot(e, vbd) for e, _ in att_p]
        fill()

        for p in range(RET_PAIRS):
            mixed_ref[rows, pl.ds(MIX_RET + p * PAIR, PAIR)] = _retention_out(
                p, rows, proj_ref, ret_out[p], ret_g).astype(BF16)
        fill()

        for p in range(ATT_PAIRS):
            mixed_ref[rows, pl.ds(MIX_ATT + p * PAIR, PAIR)] = (att_o[p] * att_p[p][1]).astype(BF16)
        fill()
        fill()
        assert next(jobs, None) is None

    _cast_blocks(cast_src, cast_dst)
    for job in _out_proj_jobs(x_ref, mixed_ref, w_out_ref, o_ref, chunk_rows[MIX_CHUNKS - 1]):
        job()


def _ffn_kernel(*refs, final_norm, n_cast):
    x_ref, g2_ref, w_in_ref, w_out_ref, gf_ref = refs[:5]
    cast_src = refs[5:5 + n_cast]
    o_ref = refs[5 + n_cast]
    cast_dst = refs[6 + n_cast:6 + 2 * n_cast]
    act_ref = refs[6 + 2 * n_cast]

    g2 = g2_ref[...]
    for sub in range(FFN_TILE // FFN_SUB):
        rows = pl.ds(sub * FFN_SUB, FFN_SUB)
        x = x_ref[rows, :]
        h = _rms_norm(x, g2).astype(BF16)
        for j in range(D_FF // FF_BLOCK):
            a = _dot(h, w_in_ref[:, pl.ds(j * FF_BLOCK, FF_BLOCK)])
            b = _dot(h, w_in_ref[:, pl.ds(D_FF + j * FF_BLOCK, FF_BLOCK)])
            act_ref[rows, pl.ds(j * FF_BLOCK, FF_BLOCK)] = (jax.nn.silu(a) * b).astype(BF16)
        if sub == 0:
            _cast_blocks(cast_src, cast_dst)
        y = x + _dot(act_ref[rows, :], w_out_ref[...])
        if final_norm:
            y = _rms_norm(y, gf_ref[...])
        o_ref[rows, :] = y


def _layer_spec(stacked, layer):
    tail = (0,) * (stacked.ndim - 1)
    return pl.BlockSpec((None,) + stacked.shape[1:], lambda *_: (layer,) + tail,
                        pipeline_mode=pl.Buffered(1))


def _const_spec(arr):
    zeros = (0,) * arr.ndim
    return pl.BlockSpec(arr.shape, lambda *_: zeros, pipeline_mode=pl.Buffered(1))


def _cast_specs(weights, layer, n_tiles):
    in_specs, out_specs, out_shapes = [], [], []
    for w in weights:
        rows, cols = w.shape[1] // n_tiles, w.shape[2]
        assert rows * n_tiles == w.shape[1] and rows % BF16_SUBLANES == 0
        in_specs.append(pl.BlockSpec((None, rows, cols), lambda i: (layer, i, 0)))
        out_specs.append(pl.BlockSpec((rows, cols), lambda i: (i, 0)))
        out_shapes.append(jax.ShapeDtypeStruct(w.shape[1:], BF16))
    return in_specs, out_specs, out_shapes


def _mixer_call(x, layer, seq, g1, w_in, w_qa, w_out, ret_g, ln_g, ln_b, ws, bs, sinks, tables,
                cast_weights, cast_layer):
    tokens = x.shape[0]
    n_tiles = tokens // MIX_TILE
    last_chunk = tokens // CHUNK - 1
    x_spec = pl.BlockSpec((MIX_TILE, D_MODEL), lambda s: (s, 0))
    next_spec = pl.BlockSpec(
        (CHUNK, D_MODEL), lambda s: (jnp.minimum((s + 1) * MIX_CHUNKS, last_chunk), 0))
    stacked = (ret_g, ln_g, ln_b, ws, bs)
    cast_in, cast_out, cast_shapes = _cast_specs(cast_weights, cast_layer, n_tiles)
    out = pl.pallas_call(
        functools.partial(_mixer_kernel, layer=layer, tiles_per_seq=seq // MIX_TILE,
                          n_cast=len(cast_weights)),
        grid=(n_tiles,),
        in_specs=[x_spec, next_spec, _layer_spec(g1, layer), _const_spec(w_in),
                  _layer_spec(w_qa, layer), _const_spec(w_out)]
        + [_layer_spec(a, layer) for a in stacked]
        + [pl.BlockSpec(memory_space=pltpu.SMEM)] + [_const_spec(a) for a in tables] + cast_in,
        out_specs=[x_spec] + cast_out,
        out_shape=[jax.ShapeDtypeStruct(x.shape, x.dtype)] + cast_shapes,
        scratch_shapes=[
            pltpu.VMEM((MIX_TILE, IN_WIDTH), F32),
            pltpu.VMEM((MIX_TILE, D_MODEL), BF16),
            pltpu.VMEM((CHUNK, MLP_GROUPS * CHUNK), BF16),
            pltpu.VMEM((RET_PAIRS, PAIR, PAIR), F32),
            pltpu.VMEM((CHUNK, PAIR), BF16),
            pltpu.VMEM((CHUNK, PAIR), BF16),
        ],
        compiler_params=pltpu.CompilerParams(
            dimension_semantics=("arbitrary",),
            vmem_limit_bytes=MIX_VMEM_BYTES),
        name="mixer",
    )(x, x, g1, w_in, w_qa, w_out, *stacked, sinks, *tables, *cast_weights)
    return out[0], out[1:]


def _ffn_call(x, layer, g2, w_in, w_out, gf, final_norm, cast_weights):
    n_tiles = x.shape[0] // FFN_TILE
    x_spec = pl.BlockSpec((FFN_TILE, D_MODEL), lambda i: (i, 0))
    cast_in, cast_out, cast_shapes = _cast_specs(cast_weights, layer + 1, n_tiles)
    out = pl.pallas_call(
        functools.partial(_ffn_kernel, final_norm=final_norm, n_cast=len(cast_weights)),
        grid=(n_tiles,),
        in_specs=[x_spec, _layer_spec(g2, layer), _const_spec(w_in), _const_spec(w_out),
                  _const_spec(gf)] + cast_in,
        out_specs=[x_spec] + cast_out,
        out_shape=[jax.ShapeDtypeStruct(x.shape, x.dtype)] + cast_shapes,
        scratch_shapes=[pltpu.VMEM((FFN_TILE, D_FF), BF16)],
        compiler_params=pltpu.CompilerParams(
            dimension_semantics=("arbitrary",),
            vmem_limit_bytes=FFN_VMEM_BYTES),
        name="ffn",
    )(x, g2, w_in, w_out, gf, *cast_weights)
    return out[0], out[1:]


def _alibi_slopes(n):
    def pow2(m):
        start = 2.0 ** (-(2.0 ** -(math.log2(m) - 3)))
        return [start * start ** i for i in range(m)]
    if math.log2(n).is_integer():
        s = pow2(n)
    else:
        c = 2 ** int(math.floor(math.log2(n)))
        s = pow2(c) + pow2(2 * c)[0::2][: n - c]
    return np.array(s, dtype=np.float32)


def _tables():
    f32 = np.float32
    scale = f32(HEAD_DIM ** -0.5)
    log_g = np.log1p(-(f32(2.0) ** (-f32(RET_DECAY_BASE) - np.arange(RET_HEADS, dtype=f32))))
    pos = np.arange(CHUNK, dtype=f32)
    diff = pos[:, None] - pos[None, :]
    intra = np.where(diff[None] >= 0,
                     np.exp(log_g[:, None, None] * np.maximum(diff, f32(0.0))[None]), f32(0.0)) * scale
    intra_tab = intra.reshape(RET_PAIRS, 2, CHUNK, CHUNK).transpose(0, 2, 1, 3).reshape(
        RET_PAIRS, CHUNK, 2 * CHUNK)
    lane_gamma = np.repeat(log_g, HEAD_DIM).reshape(RET_PAIRS, 1, PAIR)
    head_tab = np.exp(lane_gamma * (pos + f32(1.0))[None, :, None])
    tail_tab = np.swapaxes(np.exp(lane_gamma * (f32(CHUNK - 1.0) - pos)[None, :, None]) * scale, 1, 2)
    sdec_tab = np.broadcast_to(
        np.exp(lane_gamma * f32(CHUNK)).reshape(RET_PAIRS, PAIR, 1), (RET_PAIRS, PAIR, PAIR))

    qi = np.arange(CHUNK)
    kj = np.arange(2 * CHUNK)
    dist = CHUNK + qi[:, None] - kj[None, :]
    in_window = (dist >= 0) & (dist < WINDOW)
    allowed = np.stack([in_window, in_window & (kj[None, :] >= CHUNK)])
    slopes = _alibi_slopes(ATT_HEADS)[np.asarray(ATT_HEAD_ORDER)]
    bias = np.where(allowed[:, None], -(slopes[None, :, None, None] * dist.astype(f32)),
                    f32(-np.inf))
    bias_tab = bias.reshape(2, ATT_PAIRS, 2, CHUNK, 2 * CHUNK).transpose(0, 1, 3, 2, 4).reshape(
        2, ATT_PAIRS, CHUNK, 4 * CHUNK)
    tabs = (intra_tab, head_tab, tail_tab, sdec_tab, bias_tab)
    assert all(t.dtype == f32 for t in tabs)
    return tuple(jnp.asarray(np.ascontiguousarray(t)) for t in tabs)


def kernel(x, norm1_g, w_in, ret_norm_g, mlp_ln_g, mlp_ln_b, w_spatial, b_spatial, attn_sinks,
           w_out, norm2_g, w_ffn_in, w_ffn_out, final_norm_g):
    batch, seq, d_model = x.shape
    depth = w_in.shape[0]
    tokens = batch * seq
    assert d_model == D_MODEL and seq % MIX_TILE == 0 and tokens % FFN_TILE == 0

    def pair_heads(w, axis):
        shape = w.shape
        w = w.reshape(shape[:axis] + (ATT_KV_HEADS, ATT_GROUP, HEAD_DIM) + shape[axis + 1:])
        return jnp.swapaxes(w, axis, axis + 1).reshape(shape)

    w_qa = pair_heads(w_in[:, :, Q_A:K_A], 2).astype(BF16)
    att_rows = pair_heads(w_out[:, MIX_ATT:, :], 1).astype(BF16)
    mix_f32, ffn_f32 = (w_in, w_out), (w_ffn_in, w_ffn_out)
    mix_w = [w[0].astype(BF16) for w in mix_f32]
    bs_tab = jnp.repeat(jnp.swapaxes(b_spatial, 1, 2), HEAD_DIM, axis=2)
    tables = _tables()
    rows = lambda a: a.reshape(depth, 1, -1)

    x = x.reshape(tokens, d_model)
    ffn_w = None
    for l in range(depth):
        w_out_b = mix_w[1].at[MIX_ATT:, :].set(att_rows[l])
        x, cast = _mixer_call(x, l, seq, rows(norm1_g), mix_w[0], w_qa, w_out_b, rows(ret_norm_g),
                              rows(mlp_ln_g), rows(mlp_ln_b), w_spatial, bs_tab, attn_sinks, tables,
                              cast_weights=ffn_f32 if l == 0 else (), cast_layer=0)
        ffn_w = cast if l == 0 else ffn_w
        last = l == depth - 1
        x, cast = _ffn_call(x, l, rows(norm2_g), ffn_w[0], ffn_w[1], final_norm_g.reshape(1, -1),
                            final_norm=last, cast_weights=() if last else mix_f32 + ffn_f32)
        mix_w, ffn_w = cast[:2], cast[2:]
    return x.reshape(batch, seq, d_model)
```

```python
import functools
import math

import jax
import jax.numpy as jnp
import numpy as np
from jax import lax
from jax.experimental import pallas as pl
from jax.experimental.pallas import tpu as pltpu

D_MODEL = 1024
HEAD_DIM = 64
CHUNK = 128
RET_HEADS = 6
RET_WIDTH = RET_HEADS * HEAD_DIM
MLP_GROUPS = 4
MLP_WIDTH = MLP_GROUPS * HEAD_DIM
ATT_HEADS = 6
ATT_KV_HEADS = 2
ATT_WIDTH = ATT_HEADS * HEAD_DIM
ATT_KV_WIDTH = ATT_KV_HEADS * HEAD_DIM
WINDOW = 128
IN_WIDTH = 4 * RET_WIDTH + 2 * MLP_WIDTH + ATT_WIDTH + 2 * ATT_KV_WIDTH
D_FF = -(-8 * D_MODEL // (3 * 256)) * 256
RET_DECAY_BASE = 5.0
NORM_EPS = 1e-6

LANES = 128
BF16_SUBLANES = 16
PAIR = 2 * HEAD_DIM
assert PAIR == LANES and CHUNK == LANES and WINDOW == CHUNK

Q_R, K_R, V_R, G_R = 0, RET_WIDTH, 2 * RET_WIDTH, 3 * RET_WIDTH
Z_M = 4 * RET_WIDTH
Q_A = Z_M + 2 * MLP_WIDTH
K_A = Q_A + ATT_WIDTH
V_A = K_A + ATT_KV_WIDTH
MIX_RET, MIX_MLP, MIX_ATT = 0, RET_WIDTH, RET_WIDTH + MLP_WIDTH

RET_PAIRS = RET_HEADS // 2
MLP_PAIRS = MLP_GROUPS // 2
ATT_PAIRS = ATT_HEADS // 2
ATT_GROUP = ATT_HEADS // ATT_KV_HEADS
ATT_HEAD_ORDER = tuple(h for p in range(ATT_PAIRS) for h in (p, p + ATT_GROUP))

MIX_TILE = 1024
MIX_CHUNKS = MIX_TILE // CHUNK
PROJ_BLOCK = 512
FFN_TILE = 1024
FF_BLOCK = 256
MIX_VMEM_BYTES = 52 * 1024 * 1024
FFN_VMEM_BYTES = 56 * 1024 * 1024

BF16 = jnp.bfloat16
F32 = jnp.float32


def _dot(a, b):
    return jnp.dot(a, b, preferred_element_type=F32)


def _left_half(shape):
    return lax.broadcasted_iota(jnp.int32, shape, len(shape) - 1) < HEAD_DIM


def _block_diag(pair):
    left = _left_half(pair.shape)
    zero = jnp.zeros_like(pair)
    return jnp.concatenate([jnp.where(left, pair, zero), jnp.where(left, zero, pair)], axis=0)


def _block_diag_t(pair_t):
    top = lax.broadcasted_iota(jnp.int32, pair_t.shape, 0) < HEAD_DIM
    zero = jnp.zeros_like(pair_t)
    return jnp.concatenate([jnp.where(top, pair_t, zero), jnp.where(top, zero, pair_t)], axis=1)


def _rms_norm(x, g):
    ms = jnp.mean(x * x, axis=-1, keepdims=True)
    return x * lax.rsqrt(ms + NORM_EPS) * g


def _gelu(x):
    return 0.5 * x * (1.0 + lax.erf(x * np.float32(math.sqrt(0.5))))


def _col_blocks(width, block):
    return [(c, min(block, width - c)) for c in range(0, width, block)]


def _in_proj_jobs(x_rows, g1, w_in_ref, proj_ref, rows):
    h = []

    def piece(col, width):
        if not h:
            h.append(_rms_norm(x_rows(), g1).astype(BF16))
        proj_ref[rows, pl.ds(col, width)] = _dot(h[0], w_in_ref[:, pl.ds(col, width)])

    return [functools.partial(piece, c, w) for c, w in _col_blocks(IN_WIDTH, PROJ_BLOCK)]


def _out_proj_jobs(x_ref, mixed_ref, w_out_ref, o_ref, rows):
    def piece(col, width):
        cols = pl.ds(col, width)
        o_ref[rows, cols] = x_ref[rows, cols] + _dot(mixed_ref[rows, :], w_out_ref[:, cols])

    return [functools.partial(piece, c, w) for c, w in _col_blocks(D_MODEL, PROJ_BLOCK)]


def _retention_scores(p, rows, proj_ref, state_ref, head_ref, tail_ref, sdec_ref):
    q = proj_ref[rows, pl.ds(Q_R + p * PAIR, PAIR)]
    k = proj_ref[rows, pl.ds(K_R + p * PAIR, PAIR)]
    vb = proj_ref[rows, pl.ds(V_R + p * PAIR, PAIR)].astype(BF16)
    kt = k.T
    scores = _dot(q.astype(BF16), _block_diag_t(kt.astype(BF16)))
    state = state_ref[p]
    cross = _dot((q * head_ref[p]).astype(BF16), state.astype(BF16))
    kv = _dot((kt * tail_ref[p]).astype(BF16), vb)
    same_head = _left_half((PAIR, PAIR)) == (
        lax.broadcasted_iota(jnp.int32, (PAIR, PAIR), 0) < HEAD_DIM)
    state_ref[p] = state * sdec_ref[p] + jnp.where(same_head, kv, 0.0)
    return scores, cross, vb


def _retention_out(p, rows, proj_ref, out, ret_g):
    sq = out * out
    lh = _left_half(out.shape)
    ss_l = jnp.sum(jnp.where(lh, sq, 0.0), axis=-1, keepdims=True)
    ss_r = jnp.sum(jnp.where(lh, 0.0, sq), axis=-1, keepdims=True)
    ms = jnp.where(lh, ss_l, ss_r) * np.float32(1.0 / HEAD_DIM)
    ret = out * lax.rsqrt(ms + NORM_EPS) * ret_g[:, p * PAIR:(p + 1) * PAIR]
    g = proj_ref[rows, pl.ds(G_R + p * PAIR, PAIR)]
    return jax.nn.silu(g) * ret


def _softmax_pair(p, s, sink_ref, layer):
    es, inv = [], []
    for j in range(2):
        sj = s[:, j * 2 * CHUNK:(j + 1) * 2 * CHUNK]
        sink = sink_ref[layer, ATT_HEAD_ORDER[2 * p + j]]
        m = jnp.maximum(jnp.max(sj, axis=-1, keepdims=True), sink)
        e = jnp.exp(sj - m)
        es.append(e.astype(BF16))
        inv.append(1.0 / (jnp.sum(e, axis=-1, keepdims=True) + jnp.exp(sink - m)))
    return jnp.concatenate(es, axis=1), jnp.where(_left_half((CHUNK, PAIR)), inv[0], inv[1])


def _cast_blocks(cast_src, cast_dst, kinds):
    for src, dst, kind in zip(cast_src, cast_dst, kinds):
        if kind == "pair_q_cols":
            dst[:, :Q_A] = src[:, :Q_A].astype(BF16)
            qa = src[:, Q_A:K_A]
            dst[:, Q_A:K_A] = jnp.concatenate(
                [qa[:, h * HEAD_DIM:(h + 1) * HEAD_DIM] for h in ATT_HEAD_ORDER], axis=1).astype(BF16)
            dst[:, K_A:] = src[:, K_A:].astype(BF16)
        else:
            dst[...] = src[...].astype(BF16)


def _cast_kernel(*refs, kinds):
    n = len(kinds)
    _cast_blocks(refs[:n], refs[n:], kinds)


def _mixer_kernel(*refs, layer, tiles_per_seq, cast_kinds):
    (x_ref, xn_ref, g1_ref, w_in_ref, w_out_ref, ret_g_ref, ln_g_ref, ln_b_ref, ws_ref,
     bs_ref, sink_ref, intra_ref, head_ref, tail_ref, sdec_ref, bias_ref) = refs[:16]
    n_cast = len(cast_kinds)
    cast_src = refs[16:16 + n_cast]
    o_ref = refs[16 + n_cast]
    cast_dst = refs[17 + n_cast:17 + 2 * n_cast]
    proj_ref, mixed_ref, wsb_ref, state_ref, kprev_ref, vprev_ref = refs[17 + 2 * n_cast:]
    step = pl.program_id(0)
    g1 = g1_ref[...]
    chunk_rows = [pl.ds(c * CHUNK, CHUNK) for c in range(MIX_CHUNKS)]

    @pl.when(step == 0)
    def _():
        causal = lax.broadcasted_iota(jnp.int32, (CHUNK, CHUNK), 0) >= lax.broadcasted_iota(
            jnp.int32, (CHUNK, CHUNK), 1)
        for g in range(MLP_GROUPS):
            wsb_ref[:, pl.ds(g * CHUNK, CHUNK)] = jnp.where(causal, ws_ref[g], 0.0).astype(BF16)
        for job in _in_proj_jobs(lambda: x_ref[chunk_rows[0], :], g1, w_in_ref, proj_ref,
                                 chunk_rows[0]):
            job()

    seq_start = step % tiles_per_seq == 0

    @pl.when(seq_start)
    def _():
        state_ref[...] = jnp.zeros_like(state_ref)
        kprev_ref[...] = jnp.zeros_like(kprev_ref)
        vprev_ref[...] = jnp.zeros_like(vprev_ref)

    ret_g = ret_g_ref[...]
    ln_g = ln_g_ref[...]
    ln_b = ln_b_ref[...]
    first_block = seq_start.astype(jnp.int32)

    for c in range(MIX_CHUNKS):
        rows = chunk_rows[c]
        if c + 1 < MIX_CHUNKS:
            jobs = _in_proj_jobs(lambda c=c: x_ref[chunk_rows[c + 1], :], g1, w_in_ref, proj_ref,
                                 chunk_rows[c + 1])
        else:
            jobs = _in_proj_jobs(lambda: xn_ref[...], g1, w_in_ref, proj_ref, chunk_rows[0])
        if c > 0:
            jobs += _out_proj_jobs(x_ref, mixed_ref, w_out_ref, o_ref, chunk_rows[c - 1])
        jobs = iter(jobs)

        def fill():
            job = next(jobs, None)
            if job is not None:
                job()

        ret1 = [_retention_scores(p, rows, proj_ref, state_ref, head_ref, tail_ref, sdec_ref)
                for p in range(RET_PAIRS)]
        fill()

        kcur = (proj_ref[rows, pl.ds(K_A, PAIR)] * np.float32(HEAD_DIM ** -0.5)).T.astype(BF16)
        vcur = proj_ref[rows, pl.ds(V_A, PAIR)].astype(BF16)
        kbd = _block_diag_t(jnp.concatenate([kprev_ref[...], kcur], axis=1))
        vbd = _block_diag(jnp.concatenate([vprev_ref[...], vcur], axis=0))
        kprev_ref[...] = kcur
        vprev_ref[...] = vcur
        first = first_block if c == 0 else 0
        att_s = [_dot(proj_ref[rows, pl.ds(Q_A + p * PAIR, PAIR)].astype(BF16), kbd)
                 + bias_ref[first, p] for p in range(ATT_PAIRS)]
        z = _gelu(proj_ref[rows, pl.ds(Z_M, 2 * MLP_WIDTH)])
        u = z[:, :MLP_WIDTH]
        v = z[:, MLP_WIDTH:]
        mu = jnp.mean(v, axis=-1, keepdims=True)
        var = jnp.mean(jnp.square(v - mu), axis=-1, keepdims=True)
        vn = ((v - mu) * lax.rsqrt(var + NORM_EPS) * ln_g + ln_b).astype(BF16)
        fill()

        ret_out = [_dot((scores * intra_ref[p]).astype(BF16), _block_diag(vb)) + cross
                   for p, (scores, cross, vb) in enumerate(ret1)]
        fill()

        for p in range(MLP_PAIRS):
            cols = slice(p * PAIR, (p + 1) * PAIR)
            gate = _dot(wsb_ref[:, pl.ds(2 * p * CHUNK, 2 * CHUNK)], _block_diag(vn[:, cols]))
            mixed_ref[rows, pl.ds(MIX_MLP + p * PAIR, PAIR)] = (
                u[:, cols] * (gate + bs_ref[:, cols])).astype(BF16)
        att_p = [_softmax_pair(p, att_s[p], sink_ref, layer) for p in range(ATT_PAIRS)]
        fill()

        att_o = [_dot(e, vbd) for e, _ in att_p]
        fill()

        for p in range(RET_PAIRS):
            mixed_ref[rows, pl.ds(MIX_RET + p * PAIR, PAIR)] = _retention_out(
                p, rows, proj_ref, ret_out[p], ret_g).astype(BF16)
        fill()

        for p in range(ATT_PAIRS):
            mixed_ref[rows, pl.ds(MIX_ATT + p * PAIR, PAIR)] = (att_o[p] * att_p[p][1]).astype(BF16)
        fill()
        fill()
        assert next(jobs, None) is None

    _cast_blocks(cast_src, cast_dst, cast_kinds)
    for job in _out_proj_jobs(x_ref, mixed_ref, w_out_ref, o_ref, chunk_rows[MIX_CHUNKS - 1]):
        job()


def _ffn_kernel(*refs, final_norm, cast_kinds):
    n_cast = len(cast_kinds)
    x_ref, g2_ref, w_in_ref, w_out_ref, gf_ref = refs[:5]
    cast_src = refs[5:5 + n_cast]
    o_ref = refs[5 + n_cast]
    cast_dst = refs[6 + n_cast:6 + 2 * n_cast]
    act_ref = refs[6 + 2 * n_cast]

    x = x_ref[...]
    h = _rms_norm(x, g2_ref[...]).astype(BF16)
    for j in range(D_FF // FF_BLOCK):
        a = _dot(h, w_in_ref[:, pl.ds(j * FF_BLOCK, FF_BLOCK)])
        b = _dot(h, w_in_ref[:, pl.ds(D_FF + j * FF_BLOCK, FF_BLOCK)])
        act_ref[:, pl.ds(j * FF_BLOCK, FF_BLOCK)] = (jax.nn.silu(a) * b).astype(BF16)
    _cast_blocks(cast_src, cast_dst, cast_kinds)
    y = x + _dot(act_ref[...], w_out_ref[...])
    if final_norm:
        y = _rms_norm(y, gf_ref[...])
    o_ref[...] = y


def _layer_spec(stacked, layer):
    tail = (0,) * (stacked.ndim - 1)
    return pl.BlockSpec((None,) + stacked.shape[1:], lambda *_: (layer,) + tail,
                        pipeline_mode=pl.Buffered(1))


def _const_spec(arr):
    zeros = (0,) * arr.ndim
    return pl.BlockSpec(arr.shape, lambda *_: zeros, pipeline_mode=pl.Buffered(1))


CAST_KINDS = {"w_in": "pair_q_cols", "w_out": "pair_att_rows", "w_ffn_in": "plain",
              "w_ffn_out": "plain"}


def _paired_head_block(i, first):
    j = i - first
    return jnp.where(j < 0, i, first + (j % ATT_GROUP) * ATT_KV_HEADS + j // ATT_GROUP)


def _cast_specs(weights, kinds, layer, n_tiles):
    in_specs, out_specs, out_shapes = [], [], []
    for w, kind in zip(weights, kinds):
        rows, cols = w.shape[1] // n_tiles, w.shape[2]
        assert rows * n_tiles == w.shape[1] and rows % BF16_SUBLANES == 0
        in_specs.append(pl.BlockSpec((None, rows, cols), lambda i: (layer, i, 0)))
        if kind == "pair_att_rows":
            assert rows == HEAD_DIM and MIX_ATT % rows == 0
            out_specs.append(pl.BlockSpec(
                (rows, cols), lambda i: (_paired_head_block(i, MIX_ATT // HEAD_DIM), 0)))
        else:
            out_specs.append(pl.BlockSpec((rows, cols), lambda i: (i, 0)))
        out_shapes.append(jax.ShapeDtypeStruct(w.shape[1:], BF16))
    return in_specs, out_specs, out_shapes


def _cast_call(weights, kinds, layer, n_tiles):
    in_specs, out_specs, out_shapes = _cast_specs(weights, kinds, layer, n_tiles)
    return pl.pallas_call(
        functools.partial(_cast_kernel, kinds=kinds),
        grid=(n_tiles,),
        in_specs=in_specs,
        out_specs=out_specs,
        out_shape=out_shapes,
        compiler_params=pltpu.CompilerParams(dimension_semantics=("arbitrary",)),
        name="cast",
    )(*weights)


def _mixer_call(x, layer, seq, g1, w_in, w_out, ret_g, ln_g, ln_b, ws, bs, sinks, tables,
                cast_weights, cast_kinds, cast_layer):
    tokens = x.shape[0]
    n_tiles = tokens // MIX_TILE
    last_chunk = tokens // CHUNK - 1
    x_spec = pl.BlockSpec((MIX_TILE, D_MODEL), lambda s: (s, 0))
    next_spec = pl.BlockSpec(
        (CHUNK, D_MODEL), lambda s: (jnp.minimum((s + 1) * MIX_CHUNKS, last_chunk), 0))
    stacked = (ret_g, ln_g, ln_b, ws, bs)
    cast_in, cast_out, cast_shapes = _cast_specs(cast_weights, cast_kinds, cast_layer, n_tiles)
    out = pl.pallas_call(
        functools.partial(_mixer_kernel, layer=layer, tiles_per_seq=seq // MIX_TILE,
                          cast_kinds=cast_kinds),
        grid=(n_tiles,),
        in_specs=[x_spec, next_spec, _layer_spec(g1, layer), _const_spec(w_in), _const_spec(w_out)]
        + [_layer_spec(a, layer) for a in stacked]
        + [pl.BlockSpec(memory_space=pltpu.SMEM)] + [_const_spec(a) for a in tables] + cast_in,
        out_specs=[x_spec] + cast_out,
        out_shape=[jax.ShapeDtypeStruct(x.shape, x.dtype)] + cast_shapes,
        scratch_shapes=[
            pltpu.VMEM((MIX_TILE, IN_WIDTH), F32),
            pltpu.VMEM((MIX_TILE, D_MODEL), BF16),
            pltpu.VMEM((CHUNK, MLP_GROUPS * CHUNK), BF16),
            pltpu.VMEM((RET_PAIRS, PAIR, PAIR), F32),
            pltpu.VMEM((CHUNK, PAIR), BF16),
            pltpu.VMEM((CHUNK, PAIR), BF16),
        ],
        compiler_params=pltpu.CompilerParams(
            dimension_semantics=("arbitrary",),
            vmem_limit_bytes=MIX_VMEM_BYTES),
        name="mixer",
    )(x, x, g1, w_in, w_out, *stacked, sinks, *tables, *cast_weights)
    return out[0], out[1:]


def _ffn_call(x, layer, g2, w_in, w_out, gf, final_norm, cast_weights, cast_kinds):
    n_tiles = x.shape[0] // FFN_TILE
    x_spec = pl.BlockSpec((FFN_TILE, D_MODEL), lambda i: (i, 0))
    cast_in, cast_out, cast_shapes = _cast_specs(cast_weights, cast_kinds, layer + 1, n_tiles)
    out = pl.pallas_call(
        functools.partial(_ffn_kernel, final_norm=final_norm, cast_kinds=cast_kinds),
        grid=(n_tiles,),
        in_specs=[x_spec, _layer_spec(g2, layer), _const_spec(w_in), _const_spec(w_out),
                  _const_spec(gf)] + cast_in,
        out_specs=[x_spec] + cast_out,
        out_shape=[jax.ShapeDtypeStruct(x.shape, x.dtype)] + cast_shapes,
        scratch_shapes=[pltpu.VMEM((FFN_TILE, D_FF), BF16)],
        compiler_params=pltpu.CompilerParams(
            dimension_semantics=("arbitrary",),
            vmem_limit_bytes=FFN_VMEM_BYTES),
        name="ffn",
    )(x, g2, w_in, w_out, gf, *cast_weights)
    return out[0], out[1:]


def _alibi_slopes(n):
    def pow2(m):
        start = 2.0 ** (-(2.0 ** -(math.log2(m) - 3)))
        return [start * start ** i for i in range(m)]
    if math.log2(n).is_integer():
        s = pow2(n)
    else:
        c = 2 ** int(math.floor(math.log2(n)))
        s = pow2(c) + pow2(2 * c)[0::2][: n - c]
    return np.array(s, dtype=np.float32)


def _tables():
    f32 = np.float32
    scale = f32(HEAD_DIM ** -0.5)
    log_g = np.log1p(-(f32(2.0) ** (-f32(RET_DECAY_BASE) - np.arange(RET_HEADS, dtype=f32))))
    pos = np.arange(CHUNK, dtype=f32)
    diff = pos[:, None] - pos[None, :]
    intra = np.where(diff[None] >= 0,
                     np.exp(log_g[:, None, None] * np.maximum(diff, f32(0.0))[None]), f32(0.0)) * scale
    intra_tab = intra.reshape(RET_PAIRS, 2, CHUNK, CHUNK).transpose(0, 2, 1, 3).reshape(
        RET_PAIRS, CHUNK, 2 * CHUNK)
    lane_gamma = np.repeat(log_g, HEAD_DIM).reshape(RET_PAIRS, 1, PAIR)
    head_tab = np.exp(lane_gamma * (pos + f32(1.0))[None, :, None])
    tail_tab = np.swapaxes(np.exp(lane_gamma * (f32(CHUNK - 1.0) - pos)[None, :, None]) * scale, 1, 2)
    sdec_tab = np.broadcast_to(
        np.exp(lane_gamma * f32(CHUNK)).reshape(RET_PAIRS, PAIR, 1), (RET_PAIRS, PAIR, PAIR))

    qi = np.arange(CHUNK)
    kj = np.arange(2 * CHUNK)
    dist = CHUNK + qi[:, None] - kj[None, :]
    in_window = (dist >= 0) & (dist < WINDOW)
    allowed = np.stack([in_window, in_window & (kj[None, :] >= CHUNK)])
    slopes = _alibi_slopes(ATT_HEADS)[np.asarray(ATT_HEAD_ORDER)]
    bias = np.where(allowed[:, None], -(slopes[None, :, None, None] * dist.astype(f32)),
                    f32(-np.inf))
    bias_tab = bias.reshape(2, ATT_PAIRS, 2, CHUNK, 2 * CHUNK).transpose(0, 1, 3, 2, 4).reshape(
        2, ATT_PAIRS, CHUNK, 4 * CHUNK)
    tabs = (intra_tab, head_tab, tail_tab, sdec_tab, bias_tab)
    assert all(t.dtype == f32 for t in tabs)
    return tuple(jnp.asarray(np.ascontiguousarray(t)) for t in tabs)


def kernel(x, norm1_g, w_in, ret_norm_g, mlp_ln_g, mlp_ln_b, w_spatial, b_spatial, attn_sinks,
           w_out, norm2_g, w_ffn_in, w_ffn_out, final_norm_g):
    batch, seq, d_model = x.shape
    depth = w_in.shape[0]
    tokens = batch * seq
    assert d_model == D_MODEL and seq % MIX_TILE == 0 and tokens % FFN_TILE == 0

    mix_f32, ffn_f32 = (w_in, w_out), (w_ffn_in, w_ffn_out)
    mix_kinds = (CAST_KINDS["w_in"], CAST_KINDS["w_out"])
    ffn_kinds = (CAST_KINDS["w_ffn_in"], CAST_KINDS["w_ffn_out"])
    mix_w = _cast_call(mix_f32, mix_kinds, 0, tokens // MIX_TILE)
    bs_tab = jnp.repeat(jnp.swapaxes(b_spatial, 1, 2), HEAD_DIM, axis=2)
    tables = _tables()
    rows = lambda a: a.reshape(depth, 1, -1)

    x = x.reshape(tokens, d_model)
    ffn_w = None
    for l in range(depth):
        first, last = l == 0, l == depth - 1
        x, cast = _mixer_call(x, l, seq, rows(norm1_g), mix_w[0], mix_w[1], rows(ret_norm_g),
                              rows(mlp_ln_g), rows(mlp_ln_b), w_spatial, bs_tab, attn_sinks, tables,
                              cast_weights=ffn_f32 if first else (),
                              cast_kinds=ffn_kinds if first else (), cast_layer=0)
        ffn_w = cast if first else ffn_w
        x, cast = _ffn_call(x, l, rows(norm2_g), ffn_w[0], ffn_w[1], final_norm_g.reshape(1, -1),
                            final_norm=last, cast_weights=() if last else mix_f32 + ffn_f32,
                            cast_kinds=() if last else mix_kinds + ffn_kinds)
        mix_w, ffn_w = cast[:2], cast[2:]
    return x.reshape(batch, seq, d_model)
```

```python
import functools
import math

import jax
import jax.numpy as jnp
import numpy as np
from jax import lax
from jax.experimental import pallas as pl
from jax.experimental.pallas import tpu as pltpu

D_MODEL = 1024
HEAD_DIM = 64
CHUNK = 128
RET_HEADS = 6
RET_WIDTH = RET_HEADS * HEAD_DIM
MLP_GROUPS = 4
MLP_WIDTH = MLP_GROUPS * HEAD_DIM
ATT_HEADS = 6
ATT_KV_HEADS = 2
ATT_WIDTH = ATT_HEADS * HEAD_DIM
ATT_KV_WIDTH = ATT_KV_HEADS * HEAD_DIM
WINDOW = 128
IN_WIDTH = 4 * RET_WIDTH + 2 * MLP_WIDTH + ATT_WIDTH + 2 * ATT_KV_WIDTH
D_FF = -(-8 * D_MODEL // (3 * 256)) * 256
RET_DECAY_BASE = 5.0
NORM_EPS = 1e-6

LANES = 128
BF16_SUBLANES = 16
PAIR = 2 * HEAD_DIM
assert PAIR == LANES and CHUNK == LANES and WINDOW == CHUNK

Q_R, K_R, V_R, G_R = 0, RET_WIDTH, 2 * RET_WIDTH, 3 * RET_WIDTH
Z_M = 4 * RET_WIDTH
Q_A = Z_M + 2 * MLP_WIDTH
K_A = Q_A + ATT_WIDTH
V_A = K_A + ATT_KV_WIDTH
MIX_RET, MIX_MLP, MIX_ATT = 0, RET_WIDTH, RET_WIDTH + MLP_WIDTH

RET_PAIRS = RET_HEADS // 2
MLP_PAIRS = MLP_GROUPS // 2
ATT_PAIRS = ATT_HEADS // 2
ATT_GROUP = ATT_HEADS // ATT_KV_HEADS
ATT_HEAD_ORDER = tuple(h for p in range(ATT_PAIRS) for h in (p, p + ATT_GROUP))

MIX_TILE = 1024
MIX_CHUNKS = MIX_TILE // CHUNK
PROJ_BLOCK = 512
FFN_TILE = 1024
FF_BLOCK = 256
MIX_VMEM_BYTES = 52 * 1024 * 1024
FFN_VMEM_BYTES = 56 * 1024 * 1024

BF16 = jnp.bfloat16
F32 = jnp.float32


def _dot(a, b):
    return jnp.dot(a, b, preferred_element_type=F32)


def _left_half(shape):
    return lax.broadcasted_iota(jnp.int32, shape, len(shape) - 1) < HEAD_DIM


def _block_diag(pair):
    left = _left_half(pair.shape)
    zero = jnp.zeros_like(pair)
    return jnp.concatenate([jnp.where(left, pair, zero), jnp.where(left, zero, pair)], axis=0)


def _block_diag_t(pair_t):
    top = lax.broadcasted_iota(jnp.int32, pair_t.shape, 0) < HEAD_DIM
    zero = jnp.zeros_like(pair_t)
    return jnp.concatenate([jnp.where(top, pair_t, zero), jnp.where(top, zero, pair_t)], axis=1)


def _rms_norm(x, g):
    ms = jnp.mean(x * x, axis=-1, keepdims=True)
    return x * lax.rsqrt(ms + NORM_EPS) * g


def _gelu(x):
    return 0.5 * x * (1.0 + lax.erf(x * np.float32(math.sqrt(0.5))))


def _col_blocks(width, block):
    return [(c, min(block, width - c)) for c in range(0, width, block)]


def _in_proj_jobs(x_rows, g1, w_in_ref, proj_ref, rows):
    h = []

    def piece(col, width):
        if not h:
            h.append(_rms_norm(x_rows(), g1).astype(BF16))
        proj_ref[rows, pl.ds(col, width)] = _dot(h[0], w_in_ref[:, pl.ds(col, width)])

    return [functools.partial(piece, c, w) for c, w in _col_blocks(IN_WIDTH, PROJ_BLOCK)]


def _out_proj_jobs(x_ref, mixed_ref, w_out_ref, o_ref, rows):
    def piece(col, width):
        cols = pl.ds(col, width)
        o_ref[rows, cols] = x_ref[rows, cols] + _dot(mixed_ref[rows, :], w_out_ref[:, cols])

    return [functools.partial(piece, c, w) for c, w in _col_blocks(D_MODEL, PROJ_BLOCK)]


def _retention_scores(p, rows, proj_ref, state_ref, head_ref, tail_ref, sdec_ref):
    q = proj_ref[rows, pl.ds(Q_R + p * PAIR, PAIR)]
    k = proj_ref[rows, pl.ds(K_R + p * PAIR, PAIR)]
    vb = proj_ref[rows, pl.ds(V_R + p * PAIR, PAIR)].astype(BF16)
    kt = k.T
    scores = _dot(q.astype(BF16), _block_diag_t(kt.astype(BF16)))
    state = state_ref[p]
    cross = _dot((q * head_ref[p]).astype(BF16), state.astype(BF16))
    kv = _dot((kt * tail_ref[p]).astype(BF16), vb)
    same_head = _left_half((PAIR, PAIR)) == (
        lax.broadcasted_iota(jnp.int32, (PAIR, PAIR), 0) < HEAD_DIM)
    state_ref[p] = state * sdec_ref[p] + jnp.where(same_head, kv, 0.0)
    return scores, cross, vb


def _retention_out(p, rows, proj_ref, out, ret_g):
    sq = out * out
    lh = _left_half(out.shape)
    ss_l = jnp.sum(jnp.where(lh, sq, 0.0), axis=-1, keepdims=True)
    ss_r = jnp.sum(jnp.where(lh, 0.0, sq), axis=-1, keepdims=True)
    ms = jnp.where(lh, ss_l, ss_r) * np.float32(1.0 / HEAD_DIM)
    ret = out * lax.rsqrt(ms + NORM_EPS) * ret_g[:, p * PAIR:(p + 1) * PAIR]
    g = proj_ref[rows, pl.ds(G_R + p * PAIR, PAIR)]
    return jax.nn.silu(g) * ret


def _softmax_pair(p, s, sink_ref, layer):
    es, inv = [], []
    for j in range(2):
        sj = s[:, j * 2 * CHUNK:(j + 1) * 2 * CHUNK]
        sink = sink_ref[layer, ATT_HEAD_ORDER[2 * p + j]]
        m = jnp.maximum(jnp.max(sj, axis=-1, keepdims=True), sink)
        e = jnp.exp(sj - m)
        es.append(e.astype(BF16))
        inv.append(1.0 / (jnp.sum(e, axis=-1, keepdims=True) + jnp.exp(sink - m)))
    return jnp.concatenate(es, axis=1), jnp.where(_left_half((CHUNK, PAIR)), inv[0], inv[1])


def _cast_blocks(cast_src, cast_dst, kinds):
    for src, dst, kind in zip(cast_src, cast_dst, kinds):
        if kind == "pair_q_cols":
            dst[:, :Q_A] = src[:, :Q_A].astype(BF16)
            qa = src[:, Q_A:K_A]
            dst[:, Q_A:K_A] = jnp.concatenate(
                [qa[:, h * HEAD_DIM:(h + 1) * HEAD_DIM] for h in ATT_HEAD_ORDER], axis=1).astype(BF16)
            dst[:, K_A:] = src[:, K_A:].astype(BF16)
        else:
            dst[...] = src[...].astype(BF16)


def _cast_kernel(*refs, kinds):
    n = len(kinds)
    _cast_blocks(refs[:n], refs[n:], kinds)


def _mixer_kernel(*refs, layer, tiles_per_seq, cast_kinds):
    (x_ref, xn_ref, g1_ref, w_in_ref, w_out_ref, ret_g_ref, ln_g_ref, ln_b_ref, ws_ref,
     bs_ref, sink_ref, intra_ref, head_ref, tail_ref, sdec_ref, bias_ref) = refs[:16]
    n_cast = len(cast_kinds)
    cast_src = refs[16:16 + n_cast]
    o_ref = refs[16 + n_cast]
    cast_dst = refs[17 + n_cast:17 + 2 * n_cast]
    proj_ref, mixed_ref, wsb_ref, state_ref, kprev_ref, vprev_ref = refs[17 + 2 * n_cast:]
    step = pl.program_id(0)
    g1 = g1_ref[...]
    chunk_rows = [pl.ds(c * CHUNK, CHUNK) for c in range(MIX_CHUNKS)]

    @pl.when(step == 0)
    def _():
        causal = lax.broadcasted_iota(jnp.int32, (CHUNK, CHUNK), 0) >= lax.broadcasted_iota(
            jnp.int32, (CHUNK, CHUNK), 1)
        for g in range(MLP_GROUPS):
            wsb_ref[:, pl.ds(g * CHUNK, CHUNK)] = jnp.where(causal, ws_ref[g], 0.0).astype(BF16)
        for job in _in_proj_jobs(lambda: x_ref[chunk_rows[0], :], g1, w_in_ref, proj_ref,
                                 chunk_rows[0]):
            job()

    seq_start = step % tiles_per_seq == 0

    @pl.when(seq_start)
    def _():
        state_ref[...] = jnp.zeros_like(state_ref)
        kprev_ref[...] = jnp.zeros_like(kprev_ref)
        vprev_ref[...] = jnp.zeros_like(vprev_ref)

    ret_g = ret_g_ref[...]
    ln_g = ln_g_ref[...]
    ln_b = ln_b_ref[...]
    first_block = seq_start.astype(jnp.int32)

    for c in range(MIX_CHUNKS):
        rows = chunk_rows[c]
        if c + 1 < MIX_CHUNKS:
            jobs = _in_proj_jobs(lambda c=c: x_ref[chunk_rows[c + 1], :], g1, w_in_ref, proj_ref,
                                 chunk_rows[c + 1])
        else:
            jobs = _in_proj_jobs(lambda: xn_ref[...], g1, w_in_ref, proj_ref, chunk_rows[0])
        if c > 0:
            jobs += _out_proj_jobs(x_ref, mixed_ref, w_out_ref, o_ref, chunk_rows[c - 1])
        jobs = iter(jobs)

        def fill():
            job = next(jobs, None)
            if job is not None:
                job()

        ret1 = [_retention_scores(p, rows, proj_ref, state_ref, head_ref, tail_ref, sdec_ref)
                for p in range(RET_PAIRS)]
        fill()

        kcur = (proj_ref[rows, pl.ds(K_A, PAIR)] * np.float32(HEAD_DIM ** -0.5)).T.astype(BF16)
        vcur = proj_ref[rows, pl.ds(V_A, PAIR)].astype(BF16)
        kbd = _block_diag_t(jnp.concatenate([kprev_ref[...], kcur], axis=1))
        vbd = _block_diag(jnp.concatenate([vprev_ref[...], vcur], axis=0))
        kprev_ref[...] = kcur
        vprev_ref[...] = vcur
        first = first_block if c == 0 else 0
        att_s = [_dot(proj_ref[rows, pl.ds(Q_A + p * PAIR, PAIR)].astype(BF16), kbd)
                 + bias_ref[first, p] for p in range(ATT_PAIRS)]
        z = _gelu(proj_ref[rows, pl.ds(Z_M, 2 * MLP_WIDTH)])
        u = z[:, :MLP_WIDTH]
        v = z[:, MLP_WIDTH:]
        mu = jnp.mean(v, axis=-1, keepdims=True)
        var = jnp.mean(jnp.square(v - mu), axis=-1, keepdims=True)
        vn = ((v - mu) * lax.rsqrt(var + NORM_EPS) * ln_g + ln_b).astype(BF16)
        fill()

        ret_out = [_dot((scores * intra_ref[p]).astype(BF16), _block_diag(vb)) + cross
                   for p, (scores, cross, vb) in enumerate(ret1)]
        fill()

        for p in range(MLP_PAIRS):
            cols = slice(p * PAIR, (p + 1) * PAIR)
            gate = _dot(wsb_ref[:, pl.ds(2 * p * CHUNK, 2 * CHUNK)], _block_diag(vn[:, cols]))
            mixed_ref[rows, pl.ds(MIX_MLP + p * PAIR, PAIR)] = (
                u[:, cols] * (gate + bs_ref[:, cols])).astype(BF16)
        att_p = [_softmax_pair(p, att_s[p], sink_ref, layer) for p in range(ATT_PAIRS)]
        fill()

        att_o = [_dot(e, vbd) for e, _ in att_p]
        fill()

        for p in range(RET_PAIRS):
            mixed_ref[rows, pl.ds(MIX_RET + p * PAIR, PAIR)] = _retention_out(
                p, rows, proj_ref, ret_out[p], ret_g).astype(BF16)
        fill()

        for p in range(ATT_PAIRS):
            mixed_ref[rows, pl.ds(MIX_ATT + p * PAIR, PAIR)] = (att_o[p] * att_p[p][1]).astype(BF16)
        fill()
        fill()
        assert next(jobs, None) is None

    _cast_blocks(cast_src, cast_dst, cast_kinds)
    for job in _out_proj_jobs(x_ref, mixed_ref, w_out_ref, o_ref, chunk_rows[MIX_CHUNKS - 1]):
        job()


def _ffn_kernel(*refs, final_norm, cast_kinds, n_tiles):
    n_cast = len(cast_kinds)
    x_ref, g2_ref, w_in_ref, w_out_ref, gf_ref = refs[:5]
    cast_src = refs[5:5 + n_cast]
    o_ref = refs[5 + n_cast]
    cast_dst = refs[6 + n_cast:6 + 2 * n_cast]
    act_ref, x_keep_ref = refs[6 + 2 * n_cast:]
    step = pl.program_id(0)

    def gate_up():
        x = x_ref[...]
        h = _rms_norm(x, g2_ref[...]).astype(BF16)
        for j in range(D_FF // FF_BLOCK):
            a = _dot(h, w_in_ref[:, pl.ds(j * FF_BLOCK, FF_BLOCK)])
            b = _dot(h, w_in_ref[:, pl.ds(D_FF + j * FF_BLOCK, FF_BLOCK)])
            act_ref[:, pl.ds(j * FF_BLOCK, FF_BLOCK)] = (jax.nn.silu(a) * b).astype(BF16)
        x_keep_ref[...] = x
        _cast_blocks(cast_src, cast_dst, cast_kinds)

    def down():
        y = x_keep_ref[...] + _dot(act_ref[...], w_out_ref[...])
        if final_norm:
            y = _rms_norm(y, gf_ref[...])
        o_ref[...] = y

    @pl.when(step == 0)
    def _():
        gate_up()

    @pl.when((step > 0) & (step < n_tiles))
    def _():
        down()
        gate_up()

    @pl.when(step == n_tiles)
    def _():
        down()


def _layer_spec(stacked, layer):
    tail = (0,) * (stacked.ndim - 1)
    return pl.BlockSpec((None,) + stacked.shape[1:], lambda *_: (layer,) + tail,
                        pipeline_mode=pl.Buffered(1))


def _const_spec(arr):
    zeros = (0,) * arr.ndim
    return pl.BlockSpec(arr.shape, lambda *_: zeros, pipeline_mode=pl.Buffered(1))


CAST_KINDS = {"w_in": "pair_q_cols", "w_out": "pair_att_rows", "w_ffn_in": "plain",
              "w_ffn_out": "plain"}


def _paired_head_block(i, first):
    j = i - first
    return jnp.where(j < 0, i, first + (j % ATT_GROUP) * ATT_KV_HEADS + j // ATT_GROUP)


def _cast_specs(weights, kinds, layer, n_tiles, block_of_step=lambda i: i):
    in_specs, out_specs, out_shapes = [], [], []
    for w, kind in zip(weights, kinds):
        rows, cols = w.shape[1] // n_tiles, w.shape[2]
        assert rows * n_tiles == w.shape[1] and rows % BF16_SUBLANES == 0
        in_specs.append(pl.BlockSpec((None, rows, cols), lambda i: (layer, block_of_step(i), 0)))
        if kind == "pair_att_rows":
            assert rows == HEAD_DIM and MIX_ATT % rows == 0
            out_specs.append(pl.BlockSpec(
                (rows, cols),
                lambda i: (_paired_head_block(block_of_step(i), MIX_ATT // HEAD_DIM), 0)))
        else:
            out_specs.append(pl.BlockSpec((rows, cols), lambda i: (block_of_step(i), 0)))
        out_shapes.append(jax.ShapeDtypeStruct(w.shape[1:], BF16))
    return in_specs, out_specs, out_shapes


def _cast_call(weights, kinds, layer, n_tiles):
    in_specs, out_specs, out_shapes = _cast_specs(weights, kinds, layer, n_tiles)
    return pl.pallas_call(
        functools.partial(_cast_kernel, kinds=kinds),
        grid=(n_tiles,),
        in_specs=in_specs,
        out_specs=out_specs,
        out_shape=out_shapes,
        compiler_params=pltpu.CompilerParams(dimension_semantics=("arbitrary",)),
        name="cast",
    )(*weights)


def _mixer_call(x, layer, seq, g1, w_in, w_out, ret_g, ln_g, ln_b, ws, bs, sinks, tables,
                cast_weights, cast_kinds, cast_layer):
    tokens = x.shape[0]
    n_tiles = tokens // MIX_TILE
    last_chunk = tokens // CHUNK - 1
    x_spec = pl.BlockSpec((MIX_TILE, D_MODEL), lambda s: (s, 0))
    next_spec = pl.BlockSpec(
        (CHUNK, D_MODEL), lambda s: (jnp.minimum((s + 1) * MIX_CHUNKS, last_chunk), 0))
    stacked = (ret_g, ln_g, ln_b, ws, bs)
    cast_in, cast_out, cast_shapes = _cast_specs(cast_weights, cast_kinds, cast_layer, n_tiles)
    out = pl.pallas_call(
        functools.partial(_mixer_kernel, layer=layer, tiles_per_seq=seq // MIX_TILE,
                          cast_kinds=cast_kinds),
        grid=(n_tiles,),
        in_specs=[x_spec, next_spec, _layer_spec(g1, layer), _const_spec(w_in), _const_spec(w_out)]
        + [_layer_spec(a, layer) for a in stacked]
        + [pl.BlockSpec(memory_space=pltpu.SMEM)] + [_const_spec(a) for a in tables] + cast_in,
        out_specs=[x_spec] + cast_out,
        out_shape=[jax.ShapeDtypeStruct(x.shape, x.dtype)] + cast_shapes,
        scratch_shapes=[
            pltpu.VMEM((MIX_TILE, IN_WIDTH), F32),
            pltpu.VMEM((MIX_TILE, D_MODEL), BF16),
            pltpu.VMEM((CHUNK, MLP_GROUPS * CHUNK), BF16),
            pltpu.VMEM((RET_PAIRS, PAIR, PAIR), F32),
            pltpu.VMEM((CHUNK, PAIR), BF16),
            pltpu.VMEM((CHUNK, PAIR), BF16),
        ],
        compiler_params=pltpu.CompilerParams(
            dimension_semantics=("arbitrary",),
            vmem_limit_bytes=MIX_VMEM_BYTES),
        name="mixer",
    )(x, x, g1, w_in, w_out, *stacked, sinks, *tables, *cast_weights)
    return out[0], out[1:]


def _ffn_call(x, layer, g2, w_in, w_out, gf, final_norm, cast_weights, cast_kinds):
    n_tiles = x.shape[0] // FFN_TILE
    tile_in = lambda i: jnp.minimum(i, n_tiles - 1)
    tile_out = lambda i: jnp.maximum(i - 1, 0)
    in_spec = pl.BlockSpec((FFN_TILE, D_MODEL), lambda i: (tile_in(i), 0))
    out_spec = pl.BlockSpec((FFN_TILE, D_MODEL), lambda i: (tile_out(i), 0))
    cast_in, cast_out, cast_shapes = _cast_specs(cast_weights, cast_kinds, layer + 1, n_tiles,
                                                 tile_in)
    out = pl.pallas_call(
        functools.partial(_ffn_kernel, final_norm=final_norm, cast_kinds=cast_kinds,
                          n_tiles=n_tiles),
        grid=(n_tiles + 1,),
        in_specs=[in_spec, _layer_spec(g2, layer), _const_spec(w_in), _const_spec(w_out),
                  _const_spec(gf)] + cast_in,
        out_specs=[out_spec] + cast_out,
        out_shape=[jax.ShapeDtypeStruct(x.shape, x.dtype)] + cast_shapes,
        scratch_shapes=[pltpu.VMEM((FFN_TILE, D_FF), BF16), pltpu.VMEM((FFN_TILE, D_MODEL), F32)],
        compiler_params=pltpu.CompilerParams(
            dimension_semantics=("arbitrary",),
            vmem_limit_bytes=FFN_VMEM_BYTES),
        name="ffn",
    )(x, g2, w_in, w_out, gf, *cast_weights)
    return out[0], out[1:]


def _alibi_slopes(n):
    def pow2(m):
        start = 2.0 ** (-(2.0 ** -(math.log2(m) - 3)))
        return [start * start ** i for i in range(m)]
    if math.log2(n).is_integer():
        s = pow2(n)
    else:
        c = 2 ** int(math.floor(math.log2(n)))
        s = pow2(c) + pow2(2 * c)[0::2][: n - c]
    return np.array(s, dtype=np.float32)


def _tables():
    f32 = np.float32
    scale = f32(HEAD_DIM ** -0.5)
    log_g = np.log1p(-(f32(2.0) ** (-f32(RET_DECAY_BASE) - np.arange(RET_HEADS, dtype=f32))))
    pos = np.arange(CHUNK, dtype=f32)
    diff = pos[:, None] - pos[None, :]
    intra = np.where(diff[None] >= 0,
                     np.exp(log_g[:, None, None] * np.maximum(diff, f32(0.0))[None]), f32(0.0)) * scale
    intra_tab = intra.reshape(RET_PAIRS, 2, CHUNK, CHUNK).transpose(0, 2, 1, 3).reshape(
        RET_PAIRS, CHUNK, 2 * CHUNK)
    lane_gamma = np.repeat(log_g, HEAD_DIM).reshape(RET_PAIRS, 1, PAIR)
    head_tab = np.exp(lane_gamma * (pos + f32(1.0))[None, :, None])
    tail_tab = np.swapaxes(np.exp(lane_gamma * (f32(CHUNK - 1.0) - pos)[None, :, None]) * scale, 1, 2)
    sdec_tab = np.broadcast_to(
        np.exp(lane_gamma * f32(CHUNK)).reshape(RET_PAIRS, PAIR, 1), (RET_PAIRS, PAIR, PAIR))

    qi = np.arange(CHUNK)
    kj = np.arange(2 * CHUNK)
    dist = CHUNK + qi[:, None] - kj[None, :]
    in_window = (dist >= 0) & (dist < WINDOW)
    allowed = np.stack([in_window, in_window & (kj[None, :] >= CHUNK)])
    slopes = _alibi_slopes(ATT_HEADS)[np.asarray(ATT_HEAD_ORDER)]
    bias = np.where(allowed[:, None], -(slopes[None, :, None, None] * dist.astype(f32)),
                    f32(-np.inf))
    bias_tab = bias.reshape(2, ATT_PAIRS, 2, CHUNK, 2 * CHUNK).transpose(0, 1, 3, 2, 4).reshape(
        2, ATT_PAIRS, CHUNK, 4 * CHUNK)
    tabs = (intra_tab, head_tab, tail_tab, sdec_tab, bias_tab)
    assert all(t.dtype == f32 for t in tabs)
    return tuple(jnp.asarray(np.ascontiguousarray(t)) for t in tabs)


def kernel(x, norm1_g, w_in, ret_norm_g, mlp_ln_g, mlp_ln_b, w_spatial, b_spatial, attn_sinks,
           w_out, norm2_g, w_ffn_in, w_ffn_out, final_norm_g):
    batch, seq, d_model = x.shape
    depth = w_in.shape[0]
    tokens = batch * seq
    assert d_model == D_MODEL and seq % MIX_TILE == 0 and tokens % FFN_TILE == 0

    mix_f32, ffn_f32 = (w_in, w_out), (w_ffn_in, w_ffn_out)
    mix_kinds = (CAST_KINDS["w_in"], CAST_KINDS["w_out"])
    ffn_kinds = (CAST_KINDS["w_ffn_in"], CAST_KINDS["w_ffn_out"])
    mix_w = _cast_call(mix_f32, mix_kinds, 0, tokens // MIX_TILE)
    bs_tab = jnp.repeat(jnp.swapaxes(b_spatial, 1, 2), HEAD_DIM, axis=2)
    tables = _tables()
    rows = lambda a: a.reshape(depth, 1, -1)

    x = x.reshape(tokens, d_model)
    for l in range(depth):
        last = l == depth - 1
        x, ffn_w = _mixer_call(x, l, seq, rows(norm1_g), mix_w[0], mix_w[1], rows(ret_norm_g),
                               rows(mlp_ln_g), rows(mlp_ln_b), w_spatial, bs_tab, attn_sinks, tables,
                               cast_weights=ffn_f32, cast_kinds=ffn_kinds, cast_layer=l)
        x, mix_w = _ffn_call(x, l, rows(norm2_g), ffn_w[0], ffn_w[1], final_norm_g.reshape(1, -1),
                             final_norm=last, cast_weights=() if last else mix_f32,
                             cast_kinds=() if last else mix_kinds)
    return x.reshape(batch, seq, d_model)
```

```python
import functools
import math

import jax
import jax.numpy as jnp
import numpy as np
from jax import lax
from jax.experimental import pallas as pl
from jax.experimental.pallas import tpu as pltpu

D_MODEL = 1024
HEAD_DIM = 64
CHUNK = 128
RET_HEADS = 6
RET_WIDTH = RET_HEADS * HEAD_DIM
MLP_GROUPS = 4
MLP_WIDTH = MLP_GROUPS * HEAD_DIM
ATT_HEADS = 6
ATT_KV_HEADS = 2
ATT_WIDTH = ATT_HEADS * HEAD_DIM
ATT_KV_WIDTH = ATT_KV_HEADS * HEAD_DIM
WINDOW = 128
IN_WIDTH = 4 * RET_WIDTH + 2 * MLP_WIDTH + ATT_WIDTH + 2 * ATT_KV_WIDTH
D_FF = -(-8 * D_MODEL // (3 * 256)) * 256
RET_DECAY_BASE = 5.0
NORM_EPS = 1e-6

LANES = 128
BF16_SUBLANES = 16
PAIR = 2 * HEAD_DIM
assert PAIR == LANES and CHUNK == LANES and WINDOW == CHUNK

Q_R, K_R, V_R, G_R = 0, RET_WIDTH, 2 * RET_WIDTH, 3 * RET_WIDTH
Z_M = 4 * RET_WIDTH
Q_A = Z_M + 2 * MLP_WIDTH
K_A = Q_A + ATT_WIDTH
V_A = K_A + ATT_KV_WIDTH
MIX_RET, MIX_MLP, MIX_ATT = 0, RET_WIDTH, RET_WIDTH + MLP_WIDTH

RET_PAIRS = RET_HEADS // 2
MLP_PAIRS = MLP_GROUPS // 2
ATT_PAIRS = ATT_HEADS // 2
ATT_GROUP = ATT_HEADS // ATT_KV_HEADS
ATT_HEAD_ORDER = tuple(h for p in range(ATT_PAIRS) for h in (p, p + ATT_GROUP))

MIX_TILE = 1024
MIX_CHUNKS = MIX_TILE // CHUNK
PROJ_BLOCK = 512
FFN_TILE = 1024
FFN_NORM_ROWS = 256
FF_BLOCK = 256
MIX_VMEM_BYTES = 52 * 1024 * 1024
FFN_VMEM_BYTES = 56 * 1024 * 1024

BF16 = jnp.bfloat16
F32 = jnp.float32


def _dot(a, b):
    return jnp.dot(a, b, preferred_element_type=F32)


def _left_half(shape):
    return lax.broadcasted_iota(jnp.int32, shape, len(shape) - 1) < HEAD_DIM


def _block_diag(pair):
    left = _left_half(pair.shape)
    zero = jnp.zeros_like(pair)
    return jnp.concatenate([jnp.where(left, pair, zero), jnp.where(left, zero, pair)], axis=0)


def _block_diag_t(pair_t):
    top = lax.broadcasted_iota(jnp.int32, pair_t.shape, 0) < HEAD_DIM
    zero = jnp.zeros_like(pair_t)
    return jnp.concatenate([jnp.where(top, pair_t, zero), jnp.where(top, zero, pair_t)], axis=1)


def _rms_norm(x, g):
    ms = jnp.mean(x * x, axis=-1, keepdims=True)
    return x * lax.rsqrt(ms + NORM_EPS) * g


def _gelu(x):
    return 0.5 * x * (1.0 + lax.erf(x * np.float32(math.sqrt(0.5))))


def _col_blocks(width, block):
    return [(c, min(block, width - c)) for c in range(0, width, block)]


def _in_proj_jobs(x_rows, g1, w_in_ref, proj_ref, rows):
    h = []

    def piece(col, width):
        if not h:
            h.append(_rms_norm(x_rows(), g1).astype(BF16))
        proj_ref[rows, pl.ds(col, width)] = _dot(h[0], w_in_ref[:, pl.ds(col, width)])

    return [functools.partial(piece, c, w) for c, w in _col_blocks(IN_WIDTH, PROJ_BLOCK)]


def _out_proj_jobs(x_ref, mixed_ref, w_out_ref, o_ref, rows):
    def piece(col, width):
        cols = pl.ds(col, width)
        o_ref[rows, cols] = x_ref[rows, cols] + _dot(mixed_ref[rows, :], w_out_ref[:, cols])

    return [functools.partial(piece, c, w) for c, w in _col_blocks(D_MODEL, PROJ_BLOCK)]


def _retention_scores(p, rows, proj_ref, state_ref, head_ref, tail_ref, sdec_ref):
    q = proj_ref[rows, pl.ds(Q_R + p * PAIR, PAIR)]
    k = proj_ref[rows, pl.ds(K_R + p * PAIR, PAIR)]
    vb = proj_ref[rows, pl.ds(V_R + p * PAIR, PAIR)].astype(BF16)
    kt = k.T
    scores = _dot(q.astype(BF16), _block_diag_t(kt.astype(BF16)))
    state = state_ref[p]
    cross = _dot((q * head_ref[p]).astype(BF16), state.astype(BF16))
    kv = _dot((kt * tail_ref[p]).astype(BF16), vb)
    same_head = _left_half((PAIR, PAIR)) == (
        lax.broadcasted_iota(jnp.int32, (PAIR, PAIR), 0) < HEAD_DIM)
    state_ref[p] = state * sdec_ref[p] + jnp.where(same_head, kv, 0.0)
    return scores, cross, vb


def _retention_out(p, rows, proj_ref, out, ret_g):
    sq = out * out
    lh = _left_half(out.shape)
    ss_l = jnp.sum(jnp.where(lh, sq, 0.0), axis=-1, keepdims=True)
    ss_r = jnp.sum(jnp.where(lh, 0.0, sq), axis=-1, keepdims=True)
    ms = jnp.where(lh, ss_l, ss_r) * np.float32(1.0 / HEAD_DIM)
    ret = out * lax.rsqrt(ms + NORM_EPS) * ret_g[:, p * PAIR:(p + 1) * PAIR]
    g = proj_ref[rows, pl.ds(G_R + p * PAIR, PAIR)]
    return jax.nn.silu(g) * ret


def _softmax_pair(p, s, sink_ref, layer):
    es, inv = [], []
    for j in range(2):
        sj = s[:, j * 2 * CHUNK:(j + 1) * 2 * CHUNK]
        sink = sink_ref[layer, ATT_HEAD_ORDER[2 * p + j]]
        m = jnp.maximum(jnp.max(sj, axis=-1, keepdims=True), sink)
        e = jnp.exp(sj - m)
        es.append(e.astype(BF16))
        inv.append(1.0 / (jnp.sum(e, axis=-1, keepdims=True) + jnp.exp(sink - m)))
    return jnp.concatenate(es, axis=1), jnp.where(_left_half((CHUNK, PAIR)), inv[0], inv[1])


def _cast_blocks(cast_src, cast_dst, kinds):
    for src, dst, kind in zip(cast_src, cast_dst, kinds):
        if kind == "pair_q_cols":
            dst[:, :Q_A] = src[:, :Q_A].astype(BF16)
            qa = src[:, Q_A:K_A]
            dst[:, Q_A:K_A] = jnp.concatenate(
                [qa[:, h * HEAD_DIM:(h + 1) * HEAD_DIM] for h in ATT_HEAD_ORDER], axis=1).astype(BF16)
            dst[:, K_A:] = src[:, K_A:].astype(BF16)
        else:
            dst[...] = src[...].astype(BF16)


def _cast_kernel(*refs, kinds):
    n = len(kinds)
    _cast_blocks(refs[:n], refs[n:], kinds)


def _mixer_kernel(*refs, layer, tiles_per_seq, cast_kinds):
    (x_ref, xn_ref, g1_ref, w_in_ref, w_out_ref, ret_g_ref, ln_g_ref, ln_b_ref, ws_ref,
     bs_ref, sink_ref, intra_ref, head_ref, tail_ref, sdec_ref, bias_ref) = refs[:16]
    n_cast = len(cast_kinds)
    cast_src = refs[16:16 + n_cast]
    o_ref = refs[16 + n_cast]
    cast_dst = refs[17 + n_cast:17 + 2 * n_cast]
    proj_ref, mixed_ref, wsb_ref, state_ref, kprev_ref, vprev_ref = refs[17 + 2 * n_cast:]
    step = pl.program_id(0)
    g1 = g1_ref[...]
    chunk_rows = [pl.ds(c * CHUNK, CHUNK) for c in range(MIX_CHUNKS)]

    @pl.when(step == 0)
    def _():
        causal = lax.broadcasted_iota(jnp.int32, (CHUNK, CHUNK), 0) >= lax.broadcasted_iota(
            jnp.int32, (CHUNK, CHUNK), 1)
        for g in range(MLP_GROUPS):
            wsb_ref[:, pl.ds(g * CHUNK, CHUNK)] = jnp.where(causal, ws_ref[g], 0.0).astype(BF16)
        for job in _in_proj_jobs(lambda: x_ref[chunk_rows[0], :], g1, w_in_ref, proj_ref,
                                 chunk_rows[0]):
            job()

    seq_start = step % tiles_per_seq == 0

    @pl.when(seq_start)
    def _():
        state_ref[...] = jnp.zeros_like(state_ref)
        kprev_ref[...] = jnp.zeros_like(kprev_ref)
        vprev_ref[...] = jnp.zeros_like(vprev_ref)

    ret_g = ret_g_ref[...]
    ln_g = ln_g_ref[...]
    ln_b = ln_b_ref[...]
    first_block = seq_start.astype(jnp.int32)

    for c in range(MIX_CHUNKS):
        rows = chunk_rows[c]
        if c + 1 < MIX_CHUNKS:
            jobs = _in_proj_jobs(lambda c=c: x_ref[chunk_rows[c + 1], :], g1, w_in_ref, proj_ref,
                                 chunk_rows[c + 1])
        else:
            jobs = _in_proj_jobs(lambda: xn_ref[...], g1, w_in_ref, proj_ref, chunk_rows[0])
        if c > 0:
            jobs += _out_proj_jobs(x_ref, mixed_ref, w_out_ref, o_ref, chunk_rows[c - 1])
        jobs = iter(jobs)

        def fill():
            job = next(jobs, None)
            if job is not None:
                job()

        ret1 = [_retention_scores(p, rows, proj_ref, state_ref, head_ref, tail_ref, sdec_ref)
                for p in range(RET_PAIRS)]
        fill()

        kcur = (proj_ref[rows, pl.ds(K_A, PAIR)] * np.float32(HEAD_DIM ** -0.5)).T.astype(BF16)
        vcur = proj_ref[rows, pl.ds(V_A, PAIR)].astype(BF16)
        kbd = _block_diag_t(jnp.concatenate([kprev_ref[...], kcur], axis=1))
        vbd = _block_diag(jnp.concatenate([vprev_ref[...], vcur], axis=0))
        kprev_ref[...] = kcur
        vprev_ref[...] = vcur
        first = first_block if c == 0 else 0
        att_s = [_dot(proj_ref[rows, pl.ds(Q_A + p * PAIR, PAIR)].astype(BF16), kbd)
                 + bias_ref[first, p] for p in range(ATT_PAIRS)]
        z = _gelu(proj_ref[rows, pl.ds(Z_M, 2 * MLP_WIDTH)])
        u = z[:, :MLP_WIDTH]
        v = z[:, MLP_WIDTH:]
        mu = jnp.mean(v, axis=-1, keepdims=True)
        var = jnp.mean(jnp.square(v - mu), axis=-1, keepdims=True)
        vn = ((v - mu) * lax.rsqrt(var + NORM_EPS) * ln_g + ln_b).astype(BF16)
        fill()

        ret_out = [_dot((scores * intra_ref[p]).astype(BF16), _block_diag(vb)) + cross
                   for p, (scores, cross, vb) in enumerate(ret1)]
        fill()

        for p in range(MLP_PAIRS):
            cols = slice(p * PAIR, (p + 1) * PAIR)
            gate = _dot(wsb_ref[:, pl.ds(2 * p * CHUNK, 2 * CHUNK)], _block_diag(vn[:, cols]))
            mixed_ref[rows, pl.ds(MIX_MLP + p * PAIR, PAIR)] = (
                u[:, cols] * (gate + bs_ref[:, cols])).astype(BF16)
        att_p = [_softmax_pair(p, att_s[p], sink_ref, layer) for p in range(ATT_PAIRS)]
        fill()

        att_o = [_dot(e, vbd) for e, _ in att_p]
        fill()

        for p in range(RET_PAIRS):
            mixed_ref[rows, pl.ds(MIX_RET + p * PAIR, PAIR)] = _retention_out(
                p, rows, proj_ref, ret_out[p], ret_g).astype(BF16)
        fill()

        for p in range(ATT_PAIRS):
            mixed_ref[rows, pl.ds(MIX_ATT + p * PAIR, PAIR)] = (att_o[p] * att_p[p][1]).astype(BF16)
        fill()
        fill()
        assert next(jobs, None) is None

    _cast_blocks(cast_src, cast_dst, cast_kinds)
    for job in _out_proj_jobs(x_ref, mixed_ref, w_out_ref, o_ref, chunk_rows[MIX_CHUNKS - 1]):
        job()


def _ffn_kernel(*refs, final_norm, cast_kinds):
    n_cast = len(cast_kinds)
    x_ref, g2_ref, w_in_ref, w_out_ref, gf_ref = refs[:5]
    cast_src = refs[5:5 + n_cast]
    o_ref = refs[5 + n_cast]
    cast_dst = refs[6 + n_cast:6 + 2 * n_cast]
    act_ref, h_ref = refs[6 + 2 * n_cast:]

    def gate_up(h, rows, j):
        a = _dot(h, w_in_ref[:, pl.ds(j * FF_BLOCK, FF_BLOCK)])
        b = _dot(h, w_in_ref[:, pl.ds(D_FF + j * FF_BLOCK, FF_BLOCK)])
        act_ref[rows, pl.ds(j * FF_BLOCK, FF_BLOCK)] = (jax.nn.silu(a) * b).astype(BF16)

    g2 = g2_ref[...]
    for r in range(FFN_TILE // FFN_NORM_ROWS):
        rows = pl.ds(r * FFN_NORM_ROWS, FFN_NORM_ROWS)
        h = _rms_norm(x_ref[rows, :], g2).astype(BF16)
        h_ref[rows, :] = h
        gate_up(h, rows, 0)
    for j in range(1, D_FF // FF_BLOCK):
        gate_up(h_ref[...], pl.ds(0, FFN_TILE), j)
    _cast_blocks(cast_src, cast_dst, cast_kinds)
    y = x_ref[...] + _dot(act_ref[...], w_out_ref[...])
    if final_norm:
        y = _rms_norm(y, gf_ref[...])
    o_ref[...] = y


def _layer_spec(stacked, layer):
    tail = (0,) * (stacked.ndim - 1)
    return pl.BlockSpec((None,) + stacked.shape[1:], lambda *_: (layer,) + tail,
                        pipeline_mode=pl.Buffered(1))


def _const_spec(arr):
    zeros = (0,) * arr.ndim
    return pl.BlockSpec(arr.shape, lambda *_: zeros, pipeline_mode=pl.Buffered(1))


CAST_KINDS = {"w_in": "pair_q_cols", "w_out": "pair_att_rows", "w_ffn_in": "plain",
              "w_ffn_out": "plain"}


def _paired_head_block(i, first):
    j = i - first
    return jnp.where(j < 0, i, first + (j % ATT_GROUP) * ATT_KV_HEADS + j // ATT_GROUP)


def _cast_specs(weights, kinds, layer, n_tiles):
    in_specs, out_specs, out_shapes = [], [], []
    for w, kind in zip(weights, kinds):
        rows, cols = w.shape[1] // n_tiles, w.shape[2]
        assert rows * n_tiles == w.shape[1] and rows % BF16_SUBLANES == 0
        in_specs.append(pl.BlockSpec((None, rows, cols), lambda i: (layer, i, 0)))
        if kind == "pair_att_rows":
            assert rows == HEAD_DIM and MIX_ATT % rows == 0
            out_specs.append(pl.BlockSpec(
                (rows, cols), lambda i: (_paired_head_block(i, MIX_ATT // HEAD_DIM), 0)))
        else:
            out_specs.append(pl.BlockSpec((rows, cols), lambda i: (i, 0)))
        out_shapes.append(jax.ShapeDtypeStruct(w.shape[1:], BF16))
    return in_specs, out_specs, out_shapes


def _cast_call(weights, kinds, layer, n_tiles):
    in_specs, out_specs, out_shapes = _cast_specs(weights, kinds, layer, n_tiles)
    return pl.pallas_call(
        functools.partial(_cast_kernel, kinds=kinds),
        grid=(n_tiles,),
        in_specs=in_specs,
        out_specs=out_specs,
        out_shape=out_shapes,
        compiler_params=pltpu.CompilerParams(dimension_semantics=("arbitrary",)),
        name="cast",
    )(*weights)


def _mixer_call(x, layer, seq, g1, w_in, w_out, ret_g, ln_g, ln_b, ws, bs, sinks, tables,
                cast_weights, cast_kinds, cast_layer):
    tokens = x.shape[0]
    n_tiles = tokens // MIX_TILE
    last_chunk = tokens // CHUNK - 1
    x_spec = pl.BlockSpec((MIX_TILE, D_MODEL), lambda s: (s, 0))
    next_spec = pl.BlockSpec(
        (CHUNK, D_MODEL), lambda s: (jnp.minimum((s + 1) * MIX_CHUNKS, last_chunk), 0))
    stacked = (ret_g, ln_g, ln_b, ws, bs)
    cast_in, cast_out, cast_shapes = _cast_specs(cast_weights, cast_kinds, cast_layer, n_tiles)
    out = pl.pallas_call(
        functools.partial(_mixer_kernel, layer=layer, tiles_per_seq=seq // MIX_TILE,
                          cast_kinds=cast_kinds),
        grid=(n_tiles,),
        in_specs=[x_spec, next_spec, _layer_spec(g1, layer), _const_spec(w_in), _const_spec(w_out)]
        + [_layer_spec(a, layer) for a in stacked]
        + [pl.BlockSpec(memory_space=pltpu.SMEM)] + [_const_spec(a) for a in tables] + cast_in,
        out_specs=[x_spec] + cast_out,
        out_shape=[jax.ShapeDtypeStruct(x.shape, x.dtype)] + cast_shapes,
        scratch_shapes=[
            pltpu.VMEM((MIX_TILE, IN_WIDTH), F32),
            pltpu.VMEM((MIX_TILE, D_MODEL), BF16),
            pltpu.VMEM((CHUNK, MLP_GROUPS * CHUNK), BF16),
            pltpu.VMEM((RET_PAIRS, PAIR, PAIR), F32),
            pltpu.VMEM((CHUNK, PAIR), BF16),
            pltpu.VMEM((CHUNK, PAIR), BF16),
        ],
        compiler_params=pltpu.CompilerParams(
            dimension_semantics=("arbitrary",),
            vmem_limit_bytes=MIX_VMEM_BYTES),
        name="mixer",
    )(x, x, g1, w_in, w_out, *stacked, sinks, *tables, *cast_weights)
    return out[0], out[1:]


def _ffn_call(x, layer, g2, w_in, w_out, gf, final_norm, cast_weights, cast_kinds):
    n_tiles = x.shape[0] // FFN_TILE
    x_spec = pl.BlockSpec((FFN_TILE, D_MODEL), lambda i: (i, 0))
    cast_in, cast_out, cast_shapes = _cast_specs(cast_weights, cast_kinds, layer + 1, n_tiles)
    out = pl.pallas_call(
        functools.partial(_ffn_kernel, final_norm=final_norm, cast_kinds=cast_kinds),
        grid=(n_tiles,),
        in_specs=[x_spec, _layer_spec(g2, layer), _const_spec(w_in), _const_spec(w_out),
                  _const_spec(gf)] + cast_in,
        out_specs=[x_spec] + cast_out,
        out_shape=[jax.ShapeDtypeStruct(x.shape, x.dtype)] + cast_shapes,
        scratch_shapes=[pltpu.VMEM((FFN_TILE, D_FF), BF16), pltpu.VMEM((FFN_TILE, D_MODEL), BF16)],
        compiler_params=pltpu.CompilerParams(
            dimension_semantics=("arbitrary",),
            vmem_limit_bytes=FFN_VMEM_BYTES),
        name="ffn",
    )(x, g2, w_in, w_out, gf, *cast_weights)
    return out[0], out[1:]


def _alibi_slopes(n):
    def pow2(m):
        start = 2.0 ** (-(2.0 ** -(math.log2(m) - 3)))
        return [start * start ** i for i in range(m)]
    if math.log2(n).is_integer():
        s = pow2(n)
    else:
        c = 2 ** int(math.floor(math.log2(n)))
        s = pow2(c) + pow2(2 * c)[0::2][: n - c]
    return np.array(s, dtype=np.float32)


def _tables():
    f32 = np.float32
    scale = f32(HEAD_DIM ** -0.5)
    log_g = np.log1p(-(f32(2.0) ** (-f32(RET_DECAY_BASE) - np.arange(RET_HEADS, dtype=f32))))
    pos = np.arange(CHUNK, dtype=f32)
    diff = pos[:, None] - pos[None, :]
    intra = np.where(diff[None] >= 0,
                     np.exp(log_g[:, None, None] * np.maximum(diff, f32(0.0))[None]), f32(0.0)) * scale
    intra_tab = intra.reshape(RET_PAIRS, 2, CHUNK, CHUNK).transpose(0, 2, 1, 3).reshape(
        RET_PAIRS, CHUNK, 2 * CHUNK)
    lane_gamma = np.repeat(log_g, HEAD_DIM).reshape(RET_PAIRS, 1, PAIR)
    head_tab = np.exp(lane_gamma * (pos + f32(1.0))[None, :, None])
    tail_tab = np.swapaxes(np.exp(lane_gamma * (f32(CHUNK - 1.0) - pos)[None, :, None]) * scale, 1, 2)
    sdec_tab = np.broadcast_to(
        np.exp(lane_gamma * f32(CHUNK)).reshape(RET_PAIRS, PAIR, 1), (RET_PAIRS, PAIR, PAIR))

    qi = np.arange(CHUNK)
    kj = np.arange(2 * CHUNK)
    dist = CHUNK + qi[:, None] - kj[None, :]
    in_window = (dist >= 0) & (dist < WINDOW)
    allowed = np.stack([in_window, in_window & (kj[None, :] >= CHUNK)])
    slopes = _alibi_slopes(ATT_HEADS)[np.asarray(ATT_HEAD_ORDER)]
    bias = np.where(allowed[:, None], -(slopes[None, :, None, None] * dist.astype(f32)),
                    f32(-np.inf))
    bias_tab = bias.reshape(2, ATT_PAIRS, 2, CHUNK, 2 * CHUNK).transpose(0, 1, 3, 2, 4).reshape(
        2, ATT_PAIRS, CHUNK, 4 * CHUNK)
    tabs = (intra_tab, head_tab, tail_tab, sdec_tab, bias_tab)
    assert all(t.dtype == f32 for t in tabs)
    return tuple(jnp.asarray(np.ascontiguousarray(t)) for t in tabs)


def kernel(x, norm1_g, w_in, ret_norm_g, mlp_ln_g, mlp_ln_b, w_spatial, b_spatial, attn_sinks,
           w_out, norm2_g, w_ffn_in, w_ffn_out, final_norm_g):
    batch, seq, d_model = x.shape
    depth = w_in.shape[0]
    tokens = batch * seq
    assert d_model == D_MODEL and seq % MIX_TILE == 0 and tokens % FFN_TILE == 0

    mix_f32, ffn_f32 = (w_in, w_out), (w_ffn_in, w_ffn_out)
    mix_kinds = (CAST_KINDS["w_in"], CAST_KINDS["w_out"])
    ffn_kinds = (CAST_KINDS["w_ffn_in"], CAST_KINDS["w_ffn_out"])
    mix_w = _cast_call(mix_f32, mix_kinds, 0, tokens // MIX_TILE)
    bs_tab = jnp.repeat(jnp.swapaxes(b_spatial, 1, 2), HEAD_DIM, axis=2)
    tables = _tables()
    rows = lambda a: a.reshape(depth, 1, -1)

    x = x.reshape(tokens, d_model)
    ffn_w = None
    for l in range(depth):
        first, last = l == 0, l == depth - 1
        x, cast = _mixer_call(x, l, seq, rows(norm1_g), mix_w[0], mix_w[1], rows(ret_norm_g),
                              rows(mlp_ln_g), rows(mlp_ln_b), w_spatial, bs_tab, attn_sinks, tables,
                              cast_weights=ffn_f32 if first else (),
                              cast_kinds=ffn_kinds if first else (), cast_layer=0)
        ffn_w = cast if first else ffn_w
        x, cast = _ffn_call(x, l, rows(norm2_g), ffn_w[0], ffn_w[1], final_norm_g.reshape(1, -1),
                            final_norm=last, cast_weights=() if last else mix_f32 + ffn_f32,
                            cast_kinds=() if last else mix_kinds + ffn_kinds)
        mix_w, ffn_w = cast[:2], cast[2:]
    return x.reshape(batch, seq, d_model)
```

```python
import functools
import math

import jax
import jax.numpy as jnp
import numpy as np
from jax import lax
from jax.experimental import pallas as pl
from jax.experimental.pallas import tpu as pltpu

D_MODEL = 1024
HEAD_DIM = 64
CHUNK = 128
RET_HEADS = 6
RET_WIDTH = RET_HEADS * HEAD_DIM
MLP_GROUPS = 4
MLP_WIDTH = MLP_GROUPS * HEAD_DIM
ATT_HEADS = 6
ATT_KV_HEADS = 2
ATT_WIDTH = ATT_HEADS * HEAD_DIM
ATT_KV_WIDTH = ATT_KV_HEADS * HEAD_DIM
WINDOW = 128
IN_WIDTH = 4 * RET_WIDTH + 2 * MLP_WIDTH + ATT_WIDTH + 2 * ATT_KV_WIDTH
D_FF = -(-8 * D_MODEL // (3 * 256)) * 256
RET_DECAY_BASE = 5.0
NORM_EPS = 1e-6

LANES = 128
BF16_SUBLANES = 16
PAIR = 2 * HEAD_DIM
assert PAIR == LANES and CHUNK == LANES and WINDOW == CHUNK

Q_R, K_R, V_R, G_R = 0, RET_WIDTH, 2 * RET_WIDTH, 3 * RET_WIDTH
Z_M = 4 * RET_WIDTH
Q_A = Z_M + 2 * MLP_WIDTH
K_A = Q_A + ATT_WIDTH
V_A = K_A + ATT_KV_WIDTH
MIX_RET, MIX_MLP, MIX_ATT = 0, RET_WIDTH, RET_WIDTH + MLP_WIDTH

RET_PAIRS = RET_HEADS // 2
MLP_PAIRS = MLP_GROUPS // 2
ATT_PAIRS = ATT_HEADS // 2
ATT_GROUP = ATT_HEADS // ATT_KV_HEADS
ATT_HEAD_ORDER = tuple(h for p in range(ATT_PAIRS) for h in (p, p + ATT_GROUP))

MIX_TILE = 1024
MIX_CHUNKS = MIX_TILE // CHUNK
PROJ_BLOCK = 512
FFN_TILE = 1024
FFN_NORM_ROWS = 256
FF_BLOCK = 256
MIX_VMEM_BYTES = 52 * 1024 * 1024
FFN_VMEM_BYTES = 56 * 1024 * 1024

BF16 = jnp.bfloat16
F32 = jnp.float32


def _dot(a, b):
    return jnp.dot(a, b, preferred_element_type=F32)


def _left_half(shape):
    return lax.broadcasted_iota(jnp.int32, shape, len(shape) - 1) < HEAD_DIM


def _block_diag(pair):
    left = _left_half(pair.shape)
    zero = jnp.zeros_like(pair)
    return jnp.concatenate([jnp.where(left, pair, zero), jnp.where(left, zero, pair)], axis=0)


def _block_diag_t(pair_t):
    top = lax.broadcasted_iota(jnp.int32, pair_t.shape, 0) < HEAD_DIM
    zero = jnp.zeros_like(pair_t)
    return jnp.concatenate([jnp.where(top, pair_t, zero), jnp.where(top, zero, pair_t)], axis=1)


def _rms_norm(x, g):
    ms = jnp.mean(x * x, axis=-1, keepdims=True)
    return x * lax.rsqrt(ms + NORM_EPS) * g


def _gelu(x):
    return 0.5 * x * (1.0 + lax.erf(x * np.float32(math.sqrt(0.5))))


def _col_blocks(width, block):
    return [(c, min(block, width - c)) for c in range(0, width, block)]


def _in_proj_jobs(x_rows, g1, w_in_ref, proj_ref, rows):
    h = []

    def piece(col, width):
        if not h:
            h.append(_rms_norm(x_rows(), g1).astype(BF16))
        proj_ref[rows, pl.ds(col, width)] = _dot(h[0], w_in_ref[:, pl.ds(col, width)])

    return [functools.partial(piece, c, w) for c, w in _col_blocks(IN_WIDTH, PROJ_BLOCK)]


def _out_proj_jobs(x_ref, mixed_ref, w_out_ref, o_ref, rows):
    def piece(col, width):
        cols = pl.ds(col, width)
        o_ref[rows, cols] = x_ref[rows, cols] + _dot(mixed_ref[rows, :], w_out_ref[:, cols])

    return [functools.partial(piece, c, w) for c, w in _col_blocks(D_MODEL, PROJ_BLOCK)]


def _retention_scores(p, rows, proj_ref, state_ref, head_ref, tail_ref, sdec_ref):
    q = proj_ref[rows, pl.ds(Q_R + p * PAIR, PAIR)]
    k = proj_ref[rows, pl.ds(K_R + p * PAIR, PAIR)]
    vb = proj_ref[rows, pl.ds(V_R + p * PAIR, PAIR)].astype(BF16)
    kt = k.T
    scores = _dot(q.astype(BF16), _block_diag_t(kt.astype(BF16)))
    state = state_ref[p]
    cross = _dot((q * head_ref[p]).astype(BF16), state.astype(BF16))
    kv = _dot((kt * tail_ref[p]).astype(BF16), vb)
    same_head = _left_half((PAIR, PAIR)) == (
        lax.broadcasted_iota(jnp.int32, (PAIR, PAIR), 0) < HEAD_DIM)
    state_ref[p] = state * sdec_ref[p] + jnp.where(same_head, kv, 0.0)
    return scores, cross, vb


def _retention_out(p, rows, proj_ref, out, ret_g):
    sq = out * out
    lh = _left_half(out.shape)
    ss_l = jnp.sum(jnp.where(lh, sq, 0.0), axis=-1, keepdims=True)
    ss_r = jnp.sum(jnp.where(lh, 0.0, sq), axis=-1, keepdims=True)
    ms = jnp.where(lh, ss_l, ss_r) * np.float32(1.0 / HEAD_DIM)
    ret = out * lax.rsqrt(ms + NORM_EPS) * ret_g[:, p * PAIR:(p + 1) * PAIR]
    g = proj_ref[rows, pl.ds(G_R + p * PAIR, PAIR)]
    return jax.nn.silu(g) * ret


def _softmax_pair(p, s, sink_ref, layer):
    es, inv = [], []
    for j in range(2):
        sj = s[:, j * 2 * CHUNK:(j + 1) * 2 * CHUNK]
        sink = sink_ref[layer, ATT_HEAD_ORDER[2 * p + j]]
        m = jnp.maximum(jnp.max(sj, axis=-1, keepdims=True), sink)
        e = jnp.exp(sj - m)
        es.append(e.astype(BF16))
        inv.append(1.0 / (jnp.sum(e, axis=-1, keepdims=True) + jnp.exp(sink - m)))
    return jnp.concatenate(es, axis=1), jnp.where(_left_half((CHUNK, PAIR)), inv[0], inv[1])


def _cast_blocks(cast_src, cast_dst, kinds):
    for src, dst, kind in zip(cast_src, cast_dst, kinds):
        if kind == "pair_q_cols":
            dst[:, :Q_A] = src[:, :Q_A].astype(BF16)
            qa = src[:, Q_A:K_A]
            dst[:, Q_A:K_A] = jnp.concatenate(
                [qa[:, h * HEAD_DIM:(h + 1) * HEAD_DIM] for h in ATT_HEAD_ORDER], axis=1).astype(BF16)
            dst[:, K_A:] = src[:, K_A:].astype(BF16)
        else:
            dst[...] = src[...].astype(BF16)


def _cast_kernel(*refs, kinds):
    n = len(kinds)
    _cast_blocks(refs[:n], refs[n:], kinds)


def _mixer_kernel(*refs, layer, tiles_per_seq, cast_kinds):
    (x_ref, xn_ref, g1_ref, w_in_ref, w_out_ref, ret_g_ref, ln_g_ref, ln_b_ref, ws_ref,
     bs_ref, sink_ref, intra_ref, head_ref, tail_ref, sdec_ref, bias_ref) = refs[:16]
    n_cast = len(cast_kinds)
    cast_src = refs[16:16 + n_cast]
    o_ref = refs[16 + n_cast]
    cast_dst = refs[17 + n_cast:17 + 2 * n_cast]
    proj_ref, mixed_ref, wsb_ref, state_ref, kprev_ref, vprev_ref = refs[17 + 2 * n_cast:]
    step = pl.program_id(0)
    g1 = g1_ref[...]
    chunk_rows = [pl.ds(c * CHUNK, CHUNK) for c in range(MIX_CHUNKS)]

    @pl.when(step == 0)
    def _():
        causal = lax.broadcasted_iota(jnp.int32, (CHUNK, CHUNK), 0) >= lax.broadcasted_iota(
            jnp.int32, (CHUNK, CHUNK), 1)
        for g in range(MLP_GROUPS):
            wsb_ref[:, pl.ds(g * CHUNK, CHUNK)] = jnp.where(causal, ws_ref[g], 0.0).astype(BF16)
        for job in _in_proj_jobs(lambda: x_ref[chunk_rows[0], :], g1, w_in_ref, proj_ref,
                                 chunk_rows[0]):
            job()

    seq_start = step % tiles_per_seq == 0

    @pl.when(seq_start)
    def _():
        state_ref[...] = jnp.zeros_like(state_ref)
        kprev_ref[...] = jnp.zeros_like(kprev_ref)
        vprev_ref[...] = jnp.zeros_like(vprev_ref)

    ret_g = ret_g_ref[...]
    ln_g = ln_g_ref[...]
    ln_b = ln_b_ref[...]
    first_block = seq_start.astype(jnp.int32)

    def finish_previous():
        pass

    for c in range(MIX_CHUNKS):
        rows = chunk_rows[c]
        if c + 1 < MIX_CHUNKS:
            jobs = _in_proj_jobs(lambda c=c: x_ref[chunk_rows[c + 1], :], g1, w_in_ref, proj_ref,
                                 chunk_rows[c + 1])
        else:
            jobs = _in_proj_jobs(lambda: xn_ref[...], g1, w_in_ref, proj_ref, chunk_rows[0])
        jobs = iter(jobs)

        def fill():
            next(jobs)()

        ret1 = [_retention_scores(p, rows, proj_ref, state_ref, head_ref, tail_ref, sdec_ref)
                for p in range(RET_PAIRS)]
        finish_previous()
        fill()

        kcur = (proj_ref[rows, pl.ds(K_A, PAIR)] * np.float32(HEAD_DIM ** -0.5)).T.astype(BF16)
        vcur = proj_ref[rows, pl.ds(V_A, PAIR)].astype(BF16)
        kbd = _block_diag_t(jnp.concatenate([kprev_ref[...], kcur], axis=1))
        vbd = _block_diag(jnp.concatenate([vprev_ref[...], vcur], axis=0))
        kprev_ref[...] = kcur
        vprev_ref[...] = vcur
        first = first_block if c == 0 else 0
        att_s = [_dot(proj_ref[rows, pl.ds(Q_A + p * PAIR, PAIR)].astype(BF16), kbd)
                 + bias_ref[first, p] for p in range(ATT_PAIRS)]
        z = _gelu(proj_ref[rows, pl.ds(Z_M, 2 * MLP_WIDTH)])
        u = z[:, :MLP_WIDTH]
        v = z[:, MLP_WIDTH:]
        mu = jnp.mean(v, axis=-1, keepdims=True)
        var = jnp.mean(jnp.square(v - mu), axis=-1, keepdims=True)
        vn = ((v - mu) * lax.rsqrt(var + NORM_EPS) * ln_g + ln_b).astype(BF16)
        fill()

        ret_out = [_dot((scores * intra_ref[p]).astype(BF16), _block_diag(vb)) + cross
                   for p, (scores, cross, vb) in enumerate(ret1)]
        fill()

        for p in range(MLP_PAIRS):
            cols = slice(p * PAIR, (p + 1) * PAIR)
            gate = _dot(wsb_ref[:, pl.ds(2 * p * CHUNK, 2 * CHUNK)], _block_diag(vn[:, cols]))
            mixed_ref[rows, pl.ds(MIX_MLP + p * PAIR, PAIR)] = (
                u[:, cols] * (gate + bs_ref[:, cols])).astype(BF16)
        att_p = [_softmax_pair(p, att_s[p], sink_ref, layer) for p in range(ATT_PAIRS)]
        fill()

        att_o = [_dot(e, vbd) for e, _ in att_p]
        fill()
        fill()
        assert next(jobs, None) is None

        def finish_previous(rows=rows, ret_out=ret_out, att_o=att_o, att_p=att_p):
            for p in range(RET_PAIRS):
                mixed_ref[rows, pl.ds(MIX_RET + p * PAIR, PAIR)] = _retention_out(
                    p, rows, proj_ref, ret_out[p], ret_g).astype(BF16)
            for p in range(ATT_PAIRS):
                mixed_ref[rows, pl.ds(MIX_ATT + p * PAIR, PAIR)] = (
                    att_o[p] * att_p[p][1]).astype(BF16)
            for job in _out_proj_jobs(x_ref, mixed_ref, w_out_ref, o_ref, rows):
                job()

    finish_previous()
    _cast_blocks(cast_src, cast_dst, cast_kinds)


def _ffn_kernel(*refs, final_norm, cast_kinds):
    n_cast = len(cast_kinds)
    x_ref, g2_ref, w_in_ref, w_out_ref, gf_ref = refs[:5]
    cast_src = refs[5:5 + n_cast]
    o_ref = refs[5 + n_cast]
    cast_dst = refs[6 + n_cast:6 + 2 * n_cast]
    act_ref, h_ref = refs[6 + 2 * n_cast:]

    def gate_up(h, rows, j):
        a = _dot(h, w_in_ref[:, pl.ds(j * FF_BLOCK, FF_BLOCK)])
        b = _dot(h, w_in_ref[:, pl.ds(D_FF + j * FF_BLOCK, FF_BLOCK)])
        act_ref[rows, pl.ds(j * FF_BLOCK, FF_BLOCK)] = (jax.nn.silu(a) * b).astype(BF16)

    g2 = g2_ref[...]
    for r in range(FFN_TILE // FFN_NORM_ROWS):
        rows = pl.ds(r * FFN_NORM_ROWS, FFN_NORM_ROWS)
        h = _rms_norm(x_ref[rows, :], g2).astype(BF16)
        h_ref[rows, :] = h
        gate_up(h, rows, 0)
    for j in range(1, D_FF // FF_BLOCK):
        gate_up(h_ref[...], pl.ds(0, FFN_TILE), j)
    _cast_blocks(cast_src, cast_dst, cast_kinds)
    y = x_ref[...] + _dot(act_ref[...], w_out_ref[...])
    if final_norm:
        y = _rms_norm(y, gf_ref[...])
    o_ref[...] = y


def _layer_spec(stacked, layer):
    tail = (0,) * (stacked.ndim - 1)
    return pl.BlockSpec((None,) + stacked.shape[1:], lambda *_: (layer,) + tail,
                        pipeline_mode=pl.Buffered(1))


def _const_spec(arr):
    zeros = (0,) * arr.ndim
    return pl.BlockSpec(arr.shape, lambda *_: zeros, pipeline_mode=pl.Buffered(1))


CAST_KINDS = {"w_in": "pair_q_cols", "w_out": "pair_att_rows", "w_ffn_in": "plain",
              "w_ffn_out": "plain"}


def _paired_head_block(i, first):
    j = i - first
    return jnp.where(j < 0, i, first + (j % ATT_GROUP) * ATT_KV_HEADS + j // ATT_GROUP)


def _cast_specs(weights, kinds, layer, n_tiles):
    in_specs, out_specs, out_shapes = [], [], []
    for w, kind in zip(weights, kinds):
        rows, cols = w.shape[1] // n_tiles, w.shape[2]
        assert rows * n_tiles == w.shape[1] and rows % BF16_SUBLANES == 0
        in_specs.append(pl.BlockSpec((None, rows, cols), lambda i: (layer, i, 0)))
        if kind == "pair_att_rows":
            assert rows == HEAD_DIM and MIX_ATT % rows == 0
            out_specs.append(pl.BlockSpec(
                (rows, cols), lambda i: (_paired_head_block(i, MIX_ATT // HEAD_DIM), 0)))
        else:
            out_specs.append(pl.BlockSpec((rows, cols), lambda i: (i, 0)))
        out_shapes.append(jax.ShapeDtypeStruct(w.shape[1:], BF16))
    return in_specs, out_specs, out_shapes


def _cast_call(weights, kinds, layer, n_tiles):
    in_specs, out_specs, out_shapes = _cast_specs(weights, kinds, layer, n_tiles)
    return pl.pallas_call(
        functools.partial(_cast_kernel, kinds=kinds),
        grid=(n_tiles,),
        in_specs=in_specs,
        out_specs=out_specs,
        out_shape=out_shapes,
        compiler_params=pltpu.CompilerParams(dimension_semantics=("arbitrary",)),
        name="cast",
    )(*weights)


def _mixer_call(x, layer, seq, g1, w_in, w_out, ret_g, ln_g, ln_b, ws, bs, sinks, tables,
                cast_weights, cast_kinds, cast_layer):
    tokens = x.shape[0]
    n_tiles = tokens // MIX_TILE
    last_chunk = tokens // CHUNK - 1
    x_spec = pl.BlockSpec((MIX_TILE, D_MODEL), lambda s: (s, 0))
    next_spec = pl.BlockSpec(
        (CHUNK, D_MODEL), lambda s: (jnp.minimum((s + 1) * MIX_CHUNKS, last_chunk), 0))
    stacked = (ret_g, ln_g, ln_b, ws, bs)
    cast_in, cast_out, cast_shapes = _cast_specs(cast_weights, cast_kinds, cast_layer, n_tiles)
    out = pl.pallas_call(
        functools.partial(_mixer_kernel, layer=layer, tiles_per_seq=seq // MIX_TILE,
                          cast_kinds=cast_kinds),
        grid=(n_tiles,),
        in_specs=[x_spec, next_spec, _layer_spec(g1, layer), _const_spec(w_in), _const_spec(w_out)]
        + [_layer_spec(a, layer) for a in stacked]
        + [pl.BlockSpec(memory_space=pltpu.SMEM)] + [_const_spec(a) for a in tables] + cast_in,
        out_specs=[x_spec] + cast_out,
        out_shape=[jax.ShapeDtypeStruct(x.shape, x.dtype)] + cast_shapes,
        scratch_shapes=[
            pltpu.VMEM((MIX_TILE, IN_WIDTH), F32),
            pltpu.VMEM((MIX_TILE, D_MODEL), BF16),
            pltpu.VMEM((CHUNK, MLP_GROUPS * CHUNK), BF16),
            pltpu.VMEM((RET_PAIRS, PAIR, PAIR), F32),
            pltpu.VMEM((CHUNK, PAIR), BF16),
            pltpu.VMEM((CHUNK, PAIR), BF16),
        ],
        compiler_params=pltpu.CompilerParams(
            dimension_semantics=("arbitrary",),
            vmem_limit_bytes=MIX_VMEM_BYTES),
        name="mixer",
    )(x, x, g1, w_in, w_out, *stacked, sinks, *tables, *cast_weights)
    return out[0], out[1:]


def _ffn_call(x, layer, g2, w_in, w_out, gf, final_norm, cast_weights, cast_kinds):
    n_tiles = x.shape[0] // FFN_TILE
    x_spec = pl.BlockSpec((FFN_TILE, D_MODEL), lambda i: (i, 0))
    cast_in, cast_out, cast_shapes = _cast_specs(cast_weights, cast_kinds, layer + 1, n_tiles)
    out = pl.pallas_call(
        functools.partial(_ffn_kernel, final_norm=final_norm, cast_kinds=cast_kinds),
        grid=(n_tiles,),
        in_specs=[x_spec, _layer_spec(g2, layer), _const_spec(w_in), _const_spec(w_out),
                  _const_spec(gf)] + cast_in,
        out_specs=[x_spec] + cast_out,
        out_shape=[jax.ShapeDtypeStruct(x.shape, x.dtype)] + cast_shapes,
        scratch_shapes=[pltpu.VMEM((FFN_TILE, D_FF), BF16), pltpu.VMEM((FFN_TILE, D_MODEL), BF16)],
        compiler_params=pltpu.CompilerParams(
            dimension_semantics=("arbitrary",),
            vmem_limit_bytes=FFN_VMEM_BYTES),
        name="ffn",
    )(x, g2, w_in, w_out, gf, *cast_weights)
    return out[0], out[1:]


def _alibi_slopes(n):
    def pow2(m):
        start = 2.0 ** (-(2.0 ** -(math.log2(m) - 3)))
        return [start * start ** i for i in range(m)]
    if math.log2(n).is_integer():
        s = pow2(n)
    else:
        c = 2 ** int(math.floor(math.log2(n)))
        s = pow2(c) + pow2(2 * c)[0::2][: n - c]
    return np.array(s, dtype=np.float32)


def _tables():
    f32 = np.float32
    scale = f32(HEAD_DIM ** -0.5)
    log_g = np.log1p(-(f32(2.0) ** (-f32(RET_DECAY_BASE) - np.arange(RET_HEADS, dtype=f32))))
    pos = np.arange(CHUNK, dtype=f32)
    diff = pos[:, None] - pos[None, :]
    intra = np.where(diff[None] >= 0,
                     np.exp(log_g[:, None, None] * np.maximum(diff, f32(0.0))[None]), f32(0.0)) * scale
    intra_tab = intra.reshape(RET_PAIRS, 2, CHUNK, CHUNK).transpose(0, 2, 1, 3).reshape(
        RET_PAIRS, CHUNK, 2 * CHUNK)
    lane_gamma = np.repeat(log_g, HEAD_DIM).reshape(RET_PAIRS, 1, PAIR)
    head_tab = np.exp(lane_gamma * (pos + f32(1.0))[None, :, None])
    tail_tab = np.swapaxes(np.exp(lane_gamma * (f32(CHUNK - 1.0) - pos)[None, :, None]) * scale, 1, 2)
    sdec_tab = np.broadcast_to(
        np.exp(lane_gamma * f32(CHUNK)).reshape(RET_PAIRS, PAIR, 1), (RET_PAIRS, PAIR, PAIR))

    qi = np.arange(CHUNK)
    kj = np.arange(2 * CHUNK)
    dist = CHUNK + qi[:, None] - kj[None, :]
    in_window = (dist >= 0) & (dist < WINDOW)
    allowed = np.stack([in_window, in_window & (kj[None, :] >= CHUNK)])
    slopes = _alibi_slopes(ATT_HEADS)[np.asarray(ATT_HEAD_ORDER)]
    bias = np.where(allowed[:, None], -(slopes[None, :, None, None] * dist.astype(f32)),
                    f32(-np.inf))
    bias_tab = bias.reshape(2, ATT_PAIRS, 2, CHUNK, 2 * CHUNK).transpose(0, 1, 3, 2, 4).reshape(
        2, ATT_PAIRS, CHUNK, 4 * CHUNK)
    tabs = (intra_tab, head_tab, tail_tab, sdec_tab, bias_tab)
    assert all(t.dtype == f32 for t in tabs)
    return tuple(jnp.asarray(np.ascontiguousarray(t)) for t in tabs)


def kernel(x, norm1_g, w_in, ret_norm_g, mlp_ln_g, mlp_ln_b, w_spatial, b_spatial, attn_sinks,
           w_out, norm2_g, w_ffn_in, w_ffn_out, final_norm_g):
    batch, seq, d_model = x.shape
    depth = w_in.shape[0]
    tokens = batch * seq
    assert d_model == D_MODEL and seq % MIX_TILE == 0 and tokens % FFN_TILE == 0

    mix_f32, ffn_f32 = (w_in, w_out), (w_ffn_in, w_ffn_out)
    mix_kinds = (CAST_KINDS["w_in"], CAST_KINDS["w_out"])
    ffn_kinds = (CAST_KINDS["w_ffn_in"], CAST_KINDS["w_ffn_out"])
    mix_w = _cast_call(mix_f32, mix_kinds, 0, tokens // MIX_TILE)
    bs_tab = jnp.repeat(jnp.swapaxes(b_spatial, 1, 2), HEAD_DIM, axis=2)
    tables = _tables()
    rows = lambda a: a.reshape(depth, 1, -1)

    x = x.reshape(tokens, d_model)
    ffn_w = None
    for l in range(depth):
        first, last = l == 0, l == depth - 1
        x, cast = _mixer_call(x, l, seq, rows(norm1_g), mix_w[0], mix_w[1], rows(ret_norm_g),
                              rows(mlp_ln_g), rows(mlp_ln_b), w_spatial, bs_tab, attn_sinks, tables,
                              cast_weights=ffn_f32 if first else (),
                              cast_kinds=ffn_kinds if first else (), cast_layer=0)
        ffn_w = cast if first else ffn_w
        x, cast = _ffn_call(x, l, rows(norm2_g), ffn_w[0], ffn_w[1], final_norm_g.reshape(1, -1),
                            final_norm=last, cast_weights=() if last else mix_f32 + ffn_f32,
                            cast_kinds=() if last else mix_kinds + ffn_kinds)
        mix_w, ffn_w = cast[:2], cast[2:]
    return x.reshape(batch, seq, d_model)
```

```python
import functools
import math

import jax
import jax.numpy as jnp
import numpy as np
from jax import lax
from jax.experimental import pallas as pl
from jax.experimental.pallas import tpu as pltpu

D_MODEL = 1024
HEAD_DIM = 64
CHUNK = 128
RET_HEADS = 6
RET_WIDTH = RET_HEADS * HEAD_DIM
MLP_GROUPS = 4
MLP_WIDTH = MLP_GROUPS * HEAD_DIM
ATT_HEADS = 6
ATT_KV_HEADS = 2
ATT_WIDTH = ATT_HEADS * HEAD_DIM
ATT_KV_WIDTH = ATT_KV_HEADS * HEAD_DIM
WINDOW = 128
IN_WIDTH = 4 * RET_WIDTH + 2 * MLP_WIDTH + ATT_WIDTH + 2 * ATT_KV_WIDTH
D_FF = -(-8 * D_MODEL // (3 * 256)) * 256
RET_DECAY_BASE = 5.0
NORM_EPS = 1e-6

LANES = 128
BF16_SUBLANES = 16
PAIR = 2 * HEAD_DIM
assert PAIR == LANES and CHUNK == LANES and WINDOW == CHUNK

Q_R, K_R, V_R, G_R = 0, RET_WIDTH, 2 * RET_WIDTH, 3 * RET_WIDTH
Z_M = 4 * RET_WIDTH
Q_A = Z_M + 2 * MLP_WIDTH
K_A = Q_A + ATT_WIDTH
V_A = K_A + ATT_KV_WIDTH
MIX_RET, MIX_MLP, MIX_ATT = 0, RET_WIDTH, RET_WIDTH + MLP_WIDTH

RET_PAIRS = RET_HEADS // 2
MLP_PAIRS = MLP_GROUPS // 2
ATT_PAIRS = ATT_HEADS // 2
ATT_GROUP = ATT_HEADS // ATT_KV_HEADS
ATT_HEAD_ORDER = tuple(h for p in range(ATT_PAIRS) for h in (p, p + ATT_GROUP))

MIX_TILE = 1024
MIX_CHUNKS = MIX_TILE // CHUNK
PROJ_BLOCK = 512
PROJ_CHUNKS = 2
PROJ_ROWS = PROJ_CHUNKS * CHUNK
FILLS_PER_CHUNK = 3
FFN_TILE = 1024
FFN_NORM_ROWS = 256
FF_BLOCK = 256
MIX_VMEM_BYTES = 52 * 1024 * 1024
FFN_VMEM_BYTES = 56 * 1024 * 1024

BF16 = jnp.bfloat16
F32 = jnp.float32


def _dot(a, b):
    return jnp.dot(a, b, preferred_element_type=F32)


def _left_half(shape):
    return lax.broadcasted_iota(jnp.int32, shape, len(shape) - 1) < HEAD_DIM


def _block_diag(pair):
    left = _left_half(pair.shape)
    zero = jnp.zeros_like(pair)
    return jnp.concatenate([jnp.where(left, pair, zero), jnp.where(left, zero, pair)], axis=0)


def _block_diag_t(pair_t):
    top = lax.broadcasted_iota(jnp.int32, pair_t.shape, 0) < HEAD_DIM
    zero = jnp.zeros_like(pair_t)
    return jnp.concatenate([jnp.where(top, pair_t, zero), jnp.where(top, zero, pair_t)], axis=1)


def _rms_norm(x, g):
    ms = jnp.mean(x * x, axis=-1, keepdims=True)
    return x * lax.rsqrt(ms + NORM_EPS) * g


def _gelu(x):
    return 0.5 * x * (1.0 + lax.erf(x * np.float32(math.sqrt(0.5))))


def _col_blocks(width, block):
    return [(c, min(block, width - c)) for c in range(0, width, block)]


def _in_proj_jobs(x_rows, g1, w_in_ref, proj_ref, rows):
    h = []

    def piece(col, width):
        if not h:
            h.append(_rms_norm(x_rows(), g1).astype(BF16))
        proj_ref[rows, pl.ds(col, width)] = _dot(h[0], w_in_ref[:, pl.ds(col, width)])

    return [functools.partial(piece, c, w) for c, w in _col_blocks(IN_WIDTH, PROJ_BLOCK)]


def _out_proj_jobs(x_ref, mixed_ref, w_out_ref, o_ref, rows):
    def piece(col, width):
        cols = pl.ds(col, width)
        o_ref[rows, cols] = x_ref[rows, cols] + _dot(mixed_ref[rows, :], w_out_ref[:, cols])

    return [functools.partial(piece, c, w) for c, w in _col_blocks(D_MODEL, PROJ_BLOCK)]


def _retention_scores(p, rows, proj_ref, state_ref, head_ref, tail_ref, sdec_ref):
    q = proj_ref[rows, pl.ds(Q_R + p * PAIR, PAIR)]
    k = proj_ref[rows, pl.ds(K_R + p * PAIR, PAIR)]
    vb = proj_ref[rows, pl.ds(V_R + p * PAIR, PAIR)].astype(BF16)
    kt = k.T
    scores = _dot(q.astype(BF16), _block_diag_t(kt.astype(BF16)))
    state = state_ref[p]
    cross = _dot((q * head_ref[p]).astype(BF16), state.astype(BF16))
    kv = _dot((kt * tail_ref[p]).astype(BF16), vb)
    same_head = _left_half((PAIR, PAIR)) == (
        lax.broadcasted_iota(jnp.int32, (PAIR, PAIR), 0) < HEAD_DIM)
    state_ref[p] = state * sdec_ref[p] + jnp.where(same_head, kv, 0.0)
    return scores, cross, vb


def _retention_out(p, rows, proj_ref, out, ret_g):
    sq = out * out
    lh = _left_half(out.shape)
    ss_l = jnp.sum(jnp.where(lh, sq, 0.0), axis=-1, keepdims=True)
    ss_r = jnp.sum(jnp.where(lh, 0.0, sq), axis=-1, keepdims=True)
    ms = jnp.where(lh, ss_l, ss_r) * np.float32(1.0 / HEAD_DIM)
    ret = out * lax.rsqrt(ms + NORM_EPS) * ret_g[:, p * PAIR:(p + 1) * PAIR]
    g = proj_ref[rows, pl.ds(G_R + p * PAIR, PAIR)]
    return jax.nn.silu(g) * ret


def _softmax_pair(p, s, sink_ref, layer):
    es, inv = [], []
    for j in range(2):
        sj = s[:, j * 2 * CHUNK:(j + 1) * 2 * CHUNK]
        sink = sink_ref[layer, ATT_HEAD_ORDER[2 * p + j]]
        m = jnp.maximum(jnp.max(sj, axis=-1, keepdims=True), sink)
        e = jnp.exp(sj - m)
        es.append(e.astype(BF16))
        inv.append(1.0 / (jnp.sum(e, axis=-1, keepdims=True) + jnp.exp(sink - m)))
    return jnp.concatenate(es, axis=1), jnp.where(_left_half((CHUNK, PAIR)), inv[0], inv[1])


def _cast_blocks(cast_src, cast_dst, kinds):
    for src, dst, kind in zip(cast_src, cast_dst, kinds):
        if kind == "pair_q_cols":
            dst[:, :Q_A] = src[:, :Q_A].astype(BF16)
            qa = src[:, Q_A:K_A]
            dst[:, Q_A:K_A] = jnp.concatenate(
                [qa[:, h * HEAD_DIM:(h + 1) * HEAD_DIM] for h in ATT_HEAD_ORDER], axis=1).astype(BF16)
            dst[:, K_A:] = src[:, K_A:].astype(BF16)
        else:
            dst[...] = src[...].astype(BF16)


def _cast_kernel(*refs, kinds):
    n = len(kinds)
    _cast_blocks(refs[:n], refs[n:], kinds)


def _mixer_kernel(*refs, layer, tiles_per_seq, cast_kinds):
    (x_ref, xn_ref, g1_ref, w_in_ref, w_out_ref, ret_g_ref, ln_g_ref, ln_b_ref, ws_ref,
     bs_ref, sink_ref, intra_ref, head_ref, tail_ref, sdec_ref, bias_ref) = refs[:16]
    n_cast = len(cast_kinds)
    cast_src = refs[16:16 + n_cast]
    o_ref = refs[16 + n_cast]
    cast_dst = refs[17 + n_cast:17 + 2 * n_cast]
    proj_ref, mixed_ref, wsb_ref, state_ref, kprev_ref, vprev_ref = refs[17 + 2 * n_cast:]
    step = pl.program_id(0)
    g1 = g1_ref[...]
    chunk_rows = [pl.ds(c * CHUNK, CHUNK) for c in range(MIX_CHUNKS)]
    group_rows = [pl.ds(g * PROJ_ROWS, PROJ_ROWS) for g in range(MIX_TILE // PROJ_ROWS)]

    @pl.when(step == 0)
    def _():
        causal = lax.broadcasted_iota(jnp.int32, (CHUNK, CHUNK), 0) >= lax.broadcasted_iota(
            jnp.int32, (CHUNK, CHUNK), 1)
        for g in range(MLP_GROUPS):
            wsb_ref[:, pl.ds(g * CHUNK, CHUNK)] = jnp.where(causal, ws_ref[g], 0.0).astype(BF16)
        for job in _in_proj_jobs(lambda: x_ref[group_rows[0], :], g1, w_in_ref, proj_ref,
                                 group_rows[0]):
            job()

    seq_start = step % tiles_per_seq == 0

    @pl.when(seq_start)
    def _():
        state_ref[...] = jnp.zeros_like(state_ref)
        kprev_ref[...] = jnp.zeros_like(kprev_ref)
        vprev_ref[...] = jnp.zeros_like(vprev_ref)

    ret_g = ret_g_ref[...]
    ln_g = ln_g_ref[...]
    ln_b = ln_b_ref[...]
    first_block = seq_start.astype(jnp.int32)

    def finish_previous():
        pass

    for c in range(MIX_CHUNKS):
        rows = chunk_rows[c]
        if c % PROJ_CHUNKS == 0:
            g = c // PROJ_CHUNKS + 1
            if g < len(group_rows):
                jobs = _in_proj_jobs(lambda g=g: x_ref[group_rows[g], :], g1, w_in_ref, proj_ref,
                                     group_rows[g])
            else:
                jobs = _in_proj_jobs(lambda: xn_ref[...], g1, w_in_ref, proj_ref, group_rows[0])
            assert len(jobs) == PROJ_CHUNKS * FILLS_PER_CHUNK
            jobs = iter(jobs)

        def fill():
            next(jobs)()

        ret1 = [_retention_scores(p, rows, proj_ref, state_ref, head_ref, tail_ref, sdec_ref)
                for p in range(RET_PAIRS)]
        finish_previous()
        fill()

        kcur = (proj_ref[rows, pl.ds(K_A, PAIR)] * np.float32(HEAD_DIM ** -0.5)).T.astype(BF16)
        vcur = proj_ref[rows, pl.ds(V_A, PAIR)].astype(BF16)
        kbd = _block_diag_t(jnp.concatenate([kprev_ref[...], kcur], axis=1))
        vbd = _block_diag(jnp.concatenate([vprev_ref[...], vcur], axis=0))
        kprev_ref[...] = kcur
        vprev_ref[...] = vcur
        first = first_block if c == 0 else 0
        att_s = [_dot(proj_ref[rows, pl.ds(Q_A + p * PAIR, PAIR)].astype(BF16), kbd)
                 + bias_ref[first, p] for p in range(ATT_PAIRS)]
        z = _gelu(proj_ref[rows, pl.ds(Z_M, 2 * MLP_WIDTH)])
        u = z[:, :MLP_WIDTH]
        v = z[:, MLP_WIDTH:]
        mu = jnp.mean(v, axis=-1, keepdims=True)
        var = jnp.mean(jnp.square(v - mu), axis=-1, keepdims=True)
        vn = ((v - mu) * lax.rsqrt(var + NORM_EPS) * ln_g + ln_b).astype(BF16)

        ret_out = [_dot((scores * intra_ref[p]).astype(BF16), _block_diag(vb)) + cross
                   for p, (scores, cross, vb) in enumerate(ret1)]
        fill()

        for p in range(MLP_PAIRS):
            cols = slice(p * PAIR, (p + 1) * PAIR)
            gate = _dot(wsb_ref[:, pl.ds(2 * p * CHUNK, 2 * CHUNK)], _block_diag(vn[:, cols]))
            mixed_ref[rows, pl.ds(MIX_MLP + p * PAIR, PAIR)] = (
                u[:, cols] * (gate + bs_ref[:, cols])).astype(BF16)
        att_p = [_softmax_pair(p, att_s[p], sink_ref, layer) for p in range(ATT_PAIRS)]

        att_o = [_dot(e, vbd) for e, _ in att_p]
        fill()

        def finish_previous(rows=rows, ret_out=ret_out, att_o=att_o, att_p=att_p):
            for p in range(RET_PAIRS):
                mixed_ref[rows, pl.ds(MIX_RET + p * PAIR, PAIR)] = _retention_out(
                    p, rows, proj_ref, ret_out[p], ret_g).astype(BF16)
            for p in range(ATT_PAIRS):
                mixed_ref[rows, pl.ds(MIX_ATT + p * PAIR, PAIR)] = (
                    att_o[p] * att_p[p][1]).astype(BF16)
            for job in _out_proj_jobs(x_ref, mixed_ref, w_out_ref, o_ref, rows):
                job()

    finish_previous()
    _cast_blocks(cast_src, cast_dst, cast_kinds)


def _ffn_kernel(*refs, final_norm, cast_kinds):
    n_cast = len(cast_kinds)
    x_ref, g2_ref, w_in_ref, w_out_ref, gf_ref = refs[:5]
    cast_src = refs[5:5 + n_cast]
    o_ref = refs[5 + n_cast]
    cast_dst = refs[6 + n_cast:6 + 2 * n_cast]
    act_ref, h_ref = refs[6 + 2 * n_cast:]

    def gate_up(h, rows, j):
        a = _dot(h, w_in_ref[:, pl.ds(j * FF_BLOCK, FF_BLOCK)])
        b = _dot(h, w_in_ref[:, pl.ds(D_FF + j * FF_BLOCK, FF_BLOCK)])
        act_ref[rows, pl.ds(j * FF_BLOCK, FF_BLOCK)] = (jax.nn.silu(a) * b).astype(BF16)

    g2 = g2_ref[...]
    for r in range(FFN_TILE // FFN_NORM_ROWS):
        rows = pl.ds(r * FFN_NORM_ROWS, FFN_NORM_ROWS)
        h = _rms_norm(x_ref[rows, :], g2).astype(BF16)
        h_ref[rows, :] = h
        gate_up(h, rows, 0)
    for j in range(1, D_FF // FF_BLOCK):
        gate_up(h_ref[...], pl.ds(0, FFN_TILE), j)
    _cast_blocks(cast_src, cast_dst, cast_kinds)
    y = x_ref[...] + _dot(act_ref[...], w_out_ref[...])
    if final_norm:
        y = _rms_norm(y, gf_ref[...])
    o_ref[...] = y


def _layer_spec(stacked, layer):
    tail = (0,) * (stacked.ndim - 1)
    return pl.BlockSpec((None,) + stacked.shape[1:], lambda *_: (layer,) + tail,
                        pipeline_mode=pl.Buffered(1))


def _const_spec(arr):
    zeros = (0,) * arr.ndim
    return pl.BlockSpec(arr.shape, lambda *_: zeros, pipeline_mode=pl.Buffered(1))


CAST_KINDS = {"w_in": "pair_q_cols", "w_out": "pair_att_rows", "w_ffn_in": "plain",
              "w_ffn_out": "plain"}


def _paired_head_block(i, first):
    j = i - first
    return jnp.where(j < 0, i, first + (j % ATT_GROUP) * ATT_KV_HEADS + j // ATT_GROUP)


def _cast_specs(weights, kinds, layer, n_tiles):
    in_specs, out_specs, out_shapes = [], [], []
    for w, kind in zip(weights, kinds):
        rows, cols = w.shape[1] // n_tiles, w.shape[2]
        assert rows * n_tiles == w.shape[1] and rows % BF16_SUBLANES == 0
        in_specs.append(pl.BlockSpec((None, rows, cols), lambda i: (layer, i, 0)))
        if kind == "pair_att_rows":
            assert rows == HEAD_DIM and MIX_ATT % rows == 0
            out_specs.append(pl.BlockSpec(
                (rows, cols), lambda i: (_paired_head_block(i, MIX_ATT // HEAD_DIM), 0)))
        else:
            out_specs.append(pl.BlockSpec((rows, cols), lambda i: (i, 0)))
        out_shapes.append(jax.ShapeDtypeStruct(w.shape[1:], BF16))
    return in_specs, out_specs, out_shapes


def _cast_call(weights, kinds, layer, n_tiles):
    in_specs, out_specs, out_shapes = _cast_specs(weights, kinds, layer, n_tiles)
    return pl.pallas_call(
        functools.partial(_cast_kernel, kinds=kinds),
        grid=(n_tiles,),
        in_specs=in_specs,
        out_specs=out_specs,
        out_shape=out_shapes,
        compiler_params=pltpu.CompilerParams(dimension_semantics=("arbitrary",)),
        name="cast",
    )(*weights)


def _mixer_call(x, layer, seq, g1, w_in, w_out, ret_g, ln_g, ln_b, ws, bs, sinks, tables,
                cast_weights, cast_kinds, cast_layer):
    tokens = x.shape[0]
    n_tiles = tokens // MIX_TILE
    last_group = tokens // PROJ_ROWS - 1
    x_spec = pl.BlockSpec((MIX_TILE, D_MODEL), lambda s: (s, 0))
    next_spec = pl.BlockSpec(
        (PROJ_ROWS, D_MODEL),
        lambda s: (jnp.minimum((s + 1) * (MIX_TILE // PROJ_ROWS), last_group), 0))
    stacked = (ret_g, ln_g, ln_b, ws, bs)
    cast_in, cast_out, cast_shapes = _cast_specs(cast_weights, cast_kinds, cast_layer, n_tiles)
    out = pl.pallas_call(
        functools.partial(_mixer_kernel, layer=layer, tiles_per_seq=seq // MIX_TILE,
                          cast_kinds=cast_kinds),
        grid=(n_tiles,),
        in_specs=[x_spec, next_spec, _layer_spec(g1, layer), _const_spec(w_in), _const_spec(w_out)]
        + [_layer_spec(a, layer) for a in stacked]
        + [pl.BlockSpec(memory_space=pltpu.SMEM)] + [_const_spec(a) for a in tables] + cast_in,
        out_specs=[x_spec] + cast_out,
        out_shape=[jax.ShapeDtypeStruct(x.shape, x.dtype)] + cast_shapes,
        scratch_shapes=[
            pltpu.VMEM((MIX_TILE, IN_WIDTH), F32),
            pltpu.VMEM((MIX_TILE, D_MODEL), BF16),
            pltpu.VMEM((CHUNK, MLP_GROUPS * CHUNK), BF16),
            pltpu.VMEM((RET_PAIRS, PAIR, PAIR), F32),
            pltpu.VMEM((CHUNK, PAIR), BF16),
            pltpu.VMEM((CHUNK, PAIR), BF16),
        ],
        compiler_params=pltpu.CompilerParams(
            dimension_semantics=("arbitrary",),
            vmem_limit_bytes=MIX_VMEM_BYTES),
        name="mixer",
    )(x, x, g1, w_in, w_out, *stacked, sinks, *tables, *cast_weights)
    return out[0], out[1:]


def _ffn_call(x, layer, g2, w_in, w_out, gf, final_norm, cast_weights, cast_kinds):
    n_tiles = x.shape[0] // FFN_TILE
    x_spec = pl.BlockSpec((FFN_TILE, D_MODEL), lambda i: (i, 0))
    cast_in, cast_out, cast_shapes = _cast_specs(cast_weights, cast_kinds, layer + 1, n_tiles)
    out = pl.pallas_call(
        functools.partial(_ffn_kernel, final_norm=final_norm, cast_kinds=cast_kinds),
        grid=(n_tiles,),
        in_specs=[x_spec, _layer_spec(g2, layer), _const_spec(w_in), _const_spec(w_out),
                  _const_spec(gf)] + cast_in,
        out_specs=[x_spec] + cast_out,
        out_shape=[jax.ShapeDtypeStruct(x.shape, x.dtype)] + cast_shapes,
        scratch_shapes=[pltpu.VMEM((FFN_TILE, D_FF), BF16), pltpu.VMEM((FFN_TILE, D_MODEL), BF16)],
        compiler_params=pltpu.CompilerParams(
            dimension_semantics=("arbitrary",),
            vmem_limit_bytes=FFN_VMEM_BYTES),
        name="ffn",
    )(x, g2, w_in, w_out, gf, *cast_weights)
    return out[0], out[1:]


def _alibi_slopes(n):
    def pow2(m):
        start = 2.0 ** (-(2.0 ** -(math.log2(m) - 3)))
        return [start * start ** i for i in range(m)]
    if math.log2(n).is_integer():
        s = pow2(n)
    else:
        c = 2 ** int(math.floor(math.log2(n)))
        s = pow2(c) + pow2(2 * c)[0::2][: n - c]
    return np.array(s, dtype=np.float32)


def _tables():
    f32 = np.float32
    scale = f32(HEAD_DIM ** -0.5)
    log_g = np.log1p(-(f32(2.0) ** (-f32(RET_DECAY_BASE) - np.arange(RET_HEADS, dtype=f32))))
    pos = np.arange(CHUNK, dtype=f32)
    diff = pos[:, None] - pos[None, :]
    intra = np.where(diff[None] >= 0,
                     np.exp(log_g[:, None, None] * np.maximum(diff, f32(0.0))[None]), f32(0.0)) * scale
    intra_tab = intra.reshape(RET_PAIRS, 2, CHUNK, CHUNK).transpose(0, 2, 1, 3).reshape(
        RET_PAIRS, CHUNK, 2 * CHUNK)
    lane_gamma = np.repeat(log_g, HEAD_DIM).reshape(RET_PAIRS, 1, PAIR)
    head_tab = np.exp(lane_gamma * (pos + f32(1.0))[None, :, None])
    tail_tab = np.swapaxes(np.exp(lane_gamma * (f32(CHUNK - 1.0) - pos)[None, :, None]) * scale, 1, 2)
    sdec_tab = np.broadcast_to(
        np.exp(lane_gamma * f32(CHUNK)).reshape(RET_PAIRS, PAIR, 1), (RET_PAIRS, PAIR, PAIR))

    qi = np.arange(CHUNK)
    kj = np.arange(2 * CHUNK)
    dist = CHUNK + qi[:, None] - kj[None, :]
    in_window = (dist >= 0) & (dist < WINDOW)
    allowed = np.stack([in_window, in_window & (kj[None, :] >= CHUNK)])
    slopes = _alibi_slopes(ATT_HEADS)[np.asarray(ATT_HEAD_ORDER)]
    bias = np.where(allowed[:, None], -(slopes[None, :, None, None] * dist.astype(f32)),
                    f32(-np.inf))
    bias_tab = bias.reshape(2, ATT_PAIRS, 2, CHUNK, 2 * CHUNK).transpose(0, 1, 3, 2, 4).reshape(
        2, ATT_PAIRS, CHUNK, 4 * CHUNK)
    tabs = (intra_tab, head_tab, tail_tab, sdec_tab, bias_tab)
    assert all(t.dtype == f32 for t in tabs)
    return tuple(jnp.asarray(np.ascontiguousarray(t)) for t in tabs)


def kernel(x, norm1_g, w_in, ret_norm_g, mlp_ln_g, mlp_ln_b, w_spatial, b_spatial, attn_sinks,
           w_out, norm2_g, w_ffn_in, w_ffn_out, final_norm_g):
    batch, seq, d_model = x.shape
    depth = w_in.shape[0]
    tokens = batch * seq
    assert d_model == D_MODEL and seq % MIX_TILE == 0 and tokens % FFN_TILE == 0

    mix_f32, ffn_f32 = (w_in, w_out), (w_ffn_in, w_ffn_out)
    mix_kinds = (CAST_KINDS["w_in"], CAST_KINDS["w_out"])
    ffn_kinds = (CAST_KINDS["w_ffn_in"], CAST_KINDS["w_ffn_out"])
    mix_w = _cast_call(mix_f32, mix_kinds, 0, tokens // MIX_TILE)
    bs_tab = jnp.repeat(jnp.swapaxes(b_spatial, 1, 2), HEAD_DIM, axis=2)
    tables = _tables()
    rows = lambda a: a.reshape(depth, 1, -1)

    x = x.reshape(tokens, d_model)
    ffn_w = None
    for l in range(depth):
        first, last = l == 0, l == depth - 1
        x, cast = _mixer_call(x, l, seq, rows(norm1_g), mix_w[0], mix_w[1], rows(ret_norm_g),
                              rows(mlp_ln_g), rows(mlp_ln_b), w_spatial, bs_tab, attn_sinks, tables,
                              cast_weights=ffn_f32 if first else (),
                              cast_kinds=ffn_kinds if first else (), cast_layer=0)
        ffn_w = cast if first else ffn_w
        x, cast = _ffn_call(x, l, rows(norm2_g), ffn_w[0], ffn_w[1], final_norm_g.reshape(1, -1),
                            final_norm=last, cast_weights=() if last else mix_f32 + ffn_f32,
                            cast_kinds=() if last else mix_kinds + ffn_kinds)
        mix_w, ffn_w = cast[:2], cast[2:]
    return x.reshape(batch, seq, d_model)
```

```python
import functools
import math

import jax
import jax.numpy as jnp
import numpy as np
from jax import lax
from jax.experimental import pallas as pl
from jax.experimental.pallas import tpu as pltpu

D_MODEL = 1024
HEAD_DIM = 64
CHUNK = 128
RET_HEADS = 6
RET_WIDTH = RET_HEADS * HEAD_DIM
MLP_GROUPS = 4
MLP_WIDTH = MLP_GROUPS * HEAD_DIM
ATT_HEADS = 6
ATT_KV_HEADS = 2
ATT_WIDTH = ATT_HEADS * HEAD_DIM
ATT_KV_WIDTH = ATT_KV_HEADS * HEAD_DIM
WINDOW = 128
IN_WIDTH = 4 * RET_WIDTH + 2 * MLP_WIDTH + ATT_WIDTH + 2 * ATT_KV_WIDTH
D_FF = -(-8 * D_MODEL // (3 * 256)) * 256
RET_DECAY_BASE = 5.0
NORM_EPS = 1e-6

LANES = 128
BF16_SUBLANES = 16
PAIR = 2 * HEAD_DIM
assert PAIR == LANES and CHUNK == LANES and WINDOW == CHUNK

Q_R, K_R, V_R, G_R = 0, RET_WIDTH, 2 * RET_WIDTH, 3 * RET_WIDTH
Z_M = 4 * RET_WIDTH
Q_A = Z_M + 2 * MLP_WIDTH
K_A = Q_A + ATT_WIDTH
V_A = K_A + ATT_KV_WIDTH
MIX_RET, MIX_MLP, MIX_ATT = 0, RET_WIDTH, RET_WIDTH + MLP_WIDTH

RET_PAIRS = RET_HEADS // 2
MLP_PAIRS = MLP_GROUPS // 2
ATT_PAIRS = ATT_HEADS // 2
ATT_GROUP = ATT_HEADS // ATT_KV_HEADS
ATT_HEAD_ORDER = tuple(h for p in range(ATT_PAIRS) for h in (p, p + ATT_GROUP))

MIX_TILE = 1024
MIX_CHUNKS = MIX_TILE // CHUNK
PROJ_BLOCK = 512
PROJ_CHUNKS = 2
PROJ_ROWS = PROJ_CHUNKS * CHUNK
FILLS_PER_CHUNK = 3
FFN_TILE = 1024
FIRST_CAST_STEPS = 4
FFN_NORM_ROWS = 256
FF_BLOCK = 256
MIX_VMEM_BYTES = 52 * 1024 * 1024
FFN_VMEM_BYTES = 56 * 1024 * 1024

BF16 = jnp.bfloat16
F32 = jnp.float32


def _dot(a, b):
    return jnp.dot(a, b, preferred_element_type=F32)


def _left_half(shape):
    return lax.broadcasted_iota(jnp.int32, shape, len(shape) - 1) < HEAD_DIM


def _block_diag(pair):
    left = _left_half(pair.shape)
    zero = jnp.zeros_like(pair)
    return jnp.concatenate([jnp.where(left, pair, zero), jnp.where(left, zero, pair)], axis=0)


def _block_diag_t(pair_t):
    top = lax.broadcasted_iota(jnp.int32, pair_t.shape, 0) < HEAD_DIM
    zero = jnp.zeros_like(pair_t)
    return jnp.concatenate([jnp.where(top, pair_t, zero), jnp.where(top, zero, pair_t)], axis=1)


def _rms_norm(x, g):
    ms = jnp.mean(x * x, axis=-1, keepdims=True)
    return x * lax.rsqrt(ms + NORM_EPS) * g


def _gelu(x):
    return 0.5 * x * (1.0 + lax.erf(x * np.float32(math.sqrt(0.5))))


def _col_blocks(width, block):
    return [(c, min(block, width - c)) for c in range(0, width, block)]


def _in_proj_jobs(x_rows, g1, w_in_ref, proj_ref, rows):
    h = []

    def piece(col, width):
        if not h:
            h.append(_rms_norm(x_rows(), g1).astype(BF16))
        proj_ref[rows, pl.ds(col, width)] = _dot(h[0], w_in_ref[:, pl.ds(col, width)])

    return [functools.partial(piece, c, w) for c, w in _col_blocks(IN_WIDTH, PROJ_BLOCK)]


def _out_proj_jobs(x_ref, mixed_ref, w_out_ref, o_ref, rows):
    def piece(col, width):
        cols = pl.ds(col, width)
        o_ref[rows, cols] = x_ref[rows, cols] + _dot(mixed_ref[rows, :], w_out_ref[:, cols])

    return [functools.partial(piece, c, w) for c, w in _col_blocks(D_MODEL, PROJ_BLOCK)]


def _retention_scores(p, rows, proj_ref, state_ref, head_ref, tail_ref, sdec_ref):
    q = proj_ref[rows, pl.ds(Q_R + p * PAIR, PAIR)]
    k = proj_ref[rows, pl.ds(K_R + p * PAIR, PAIR)]
    vb = proj_ref[rows, pl.ds(V_R + p * PAIR, PAIR)].astype(BF16)
    kt = k.T
    scores = _dot(q.astype(BF16), _block_diag_t(kt.astype(BF16)))
    state = state_ref[p]
    cross = _dot((q * head_ref[p]).astype(BF16), state.astype(BF16))
    kv = _dot((kt * tail_ref[p]).astype(BF16), vb)
    same_head = _left_half((PAIR, PAIR)) == (
        lax.broadcasted_iota(jnp.int32, (PAIR, PAIR), 0) < HEAD_DIM)
    state_ref[p] = state * sdec_ref[p] + jnp.where(same_head, kv, 0.0)
    return scores, cross, vb


def _retention_out(p, rows, proj_ref, out, ret_g):
    sq = out * out
    lh = _left_half(out.shape)
    ss_l = jnp.sum(jnp.where(lh, sq, 0.0), axis=-1, keepdims=True)
    ss_r = jnp.sum(jnp.where(lh, 0.0, sq), axis=-1, keepdims=True)
    ms = jnp.where(lh, ss_l, ss_r) * np.float32(1.0 / HEAD_DIM)
    ret = out * lax.rsqrt(ms + NORM_EPS) * ret_g[:, p * PAIR:(p + 1) * PAIR]
    g = proj_ref[rows, pl.ds(G_R + p * PAIR, PAIR)]
    return jax.nn.silu(g) * ret


def _softmax_pair(p, s, sink_ref, layer):
    es, inv = [], []
    for j in range(2):
        sj = s[:, j * 2 * CHUNK:(j + 1) * 2 * CHUNK]
        sink = sink_ref[layer, ATT_HEAD_ORDER[2 * p + j]]
        m = jnp.maximum(jnp.max(sj, axis=-1, keepdims=True), sink)
        e = jnp.exp(sj - m)
        es.append(e.astype(BF16))
        inv.append(1.0 / (jnp.sum(e, axis=-1, keepdims=True) + jnp.exp(sink - m)))
    return jnp.concatenate(es, axis=1), jnp.where(_left_half((CHUNK, PAIR)), inv[0], inv[1])


def _cast_blocks(cast_src, cast_dst, kinds):
    for src, dst, kind in zip(cast_src, cast_dst, kinds):
        if kind == "pair_q_cols":
            dst[:, :Q_A] = src[:, :Q_A].astype(BF16)
            qa = src[:, Q_A:K_A]
            dst[:, Q_A:K_A] = jnp.concatenate(
                [qa[:, h * HEAD_DIM:(h + 1) * HEAD_DIM] for h in ATT_HEAD_ORDER], axis=1).astype(BF16)
            dst[:, K_A:] = src[:, K_A:].astype(BF16)
        else:
            dst[...] = src[...].astype(BF16)


def _first_cast_kernel(w_in_ref, w_out_ref, w_in_dst, w_out_dst):
    _cast_blocks((w_in_ref,), (w_in_dst,), (CAST_KINDS["w_in"],))

    @pl.when(pl.program_id(0) == 0)
    def _():
        w_out_dst[:MIX_ATT, :] = w_out_ref[:MIX_ATT, :].astype(BF16)
        for q, h in enumerate(ATT_HEAD_ORDER):
            w_out_dst[pl.ds(MIX_ATT + q * HEAD_DIM, HEAD_DIM), :] = (
                w_out_ref[pl.ds(MIX_ATT + h * HEAD_DIM, HEAD_DIM), :].astype(BF16))


def _mixer_kernel(*refs, layer, tiles_per_seq, cast_kinds):
    (x_ref, xn_ref, g1_ref, w_in_ref, w_out_ref, ret_g_ref, ln_g_ref, ln_b_ref, ws_ref,
     bs_ref, sink_ref, intra_ref, head_ref, tail_ref, sdec_ref, bias_ref) = refs[:16]
    n_cast = len(cast_kinds)
    cast_src = refs[16:16 + n_cast]
    o_ref = refs[16 + n_cast]
    cast_dst = refs[17 + n_cast:17 + 2 * n_cast]
    proj_ref, mixed_ref, wsb_ref, state_ref, kprev_ref, vprev_ref = refs[17 + 2 * n_cast:]
    step = pl.program_id(0)
    g1 = g1_ref[...]
    chunk_rows = [pl.ds(c * CHUNK, CHUNK) for c in range(MIX_CHUNKS)]
    group_rows = [pl.ds(g * PROJ_ROWS, PROJ_ROWS) for g in range(MIX_TILE // PROJ_ROWS)]

    @pl.when(step == 0)
    def _():
        causal = lax.broadcasted_iota(jnp.int32, (CHUNK, CHUNK), 0) >= lax.broadcasted_iota(
            jnp.int32, (CHUNK, CHUNK), 1)
        for g in range(MLP_GROUPS):
            wsb_ref[:, pl.ds(g * CHUNK, CHUNK)] = jnp.where(causal, ws_ref[g], 0.0).astype(BF16)
        for job in _in_proj_jobs(lambda: x_ref[group_rows[0], :], g1, w_in_ref, proj_ref,
                                 group_rows[0]):
            job()

    seq_start = step % tiles_per_seq == 0

    @pl.when(seq_start)
    def _():
        state_ref[...] = jnp.zeros_like(state_ref)
        kprev_ref[...] = jnp.zeros_like(kprev_ref)
        vprev_ref[...] = jnp.zeros_like(vprev_ref)

    ret_g = ret_g_ref[...]
    ln_g = ln_g_ref[...]
    ln_b = ln_b_ref[...]
    first_block = seq_start.astype(jnp.int32)

    def finish_previous():
        pass

    for c in range(MIX_CHUNKS):
        rows = chunk_rows[c]
        if c % PROJ_CHUNKS == 0:
            g = c // PROJ_CHUNKS + 1
            if g < len(group_rows):
                jobs = _in_proj_jobs(lambda g=g: x_ref[group_rows[g], :], g1, w_in_ref, proj_ref,
                                     group_rows[g])
            else:
                jobs = _in_proj_jobs(lambda: xn_ref[...], g1, w_in_ref, proj_ref, group_rows[0])
            assert len(jobs) == PROJ_CHUNKS * FILLS_PER_CHUNK
            jobs = iter(jobs)

        def fill():
            next(jobs)()

        ret1 = [_retention_scores(p, rows, proj_ref, state_ref, head_ref, tail_ref, sdec_ref)
                for p in range(RET_PAIRS)]
        finish_previous()
        fill()

        kcur = (proj_ref[rows, pl.ds(K_A, PAIR)] * np.float32(HEAD_DIM ** -0.5)).T.astype(BF16)
        vcur = proj_ref[rows, pl.ds(V_A, PAIR)].astype(BF16)
        kbd = _block_diag_t(jnp.concatenate([kprev_ref[...], kcur], axis=1))
        vbd = _block_diag(jnp.concatenate([vprev_ref[...], vcur], axis=0))
        kprev_ref[...] = kcur
        vprev_ref[...] = vcur
        first = first_block if c == 0 else 0
        att_s = [_dot(proj_ref[rows, pl.ds(Q_A + p * PAIR, PAIR)].astype(BF16), kbd)
                 + bias_ref[first, p] for p in range(ATT_PAIRS)]
        z = _gelu(proj_ref[rows, pl.ds(Z_M, 2 * MLP_WIDTH)])
        u = z[:, :MLP_WIDTH]
        v = z[:, MLP_WIDTH:]
        mu = jnp.mean(v, axis=-1, keepdims=True)
        var = jnp.mean(jnp.square(v - mu), axis=-1, keepdims=True)
        vn = ((v - mu) * lax.rsqrt(var + NORM_EPS) * ln_g + ln_b).astype(BF16)

        ret_out = [_dot((scores * intra_ref[p]).astype(BF16), _block_diag(vb)) + cross
                   for p, (scores, cross, vb) in enumerate(ret1)]
        fill()

        for p in range(MLP_PAIRS):
            cols = slice(p * PAIR, (p + 1) * PAIR)
            gate = _dot(wsb_ref[:, pl.ds(2 * p * CHUNK, 2 * CHUNK)], _block_diag(vn[:, cols]))
            mixed_ref[rows, pl.ds(MIX_MLP + p * PAIR, PAIR)] = (
                u[:, cols] * (gate + bs_ref[:, cols])).astype(BF16)
        att_p = [_softmax_pair(p, att_s[p], sink_ref, layer) for p in range(ATT_PAIRS)]

        att_o = [_dot(e, vbd) for e, _ in att_p]
        fill()

        def finish_previous(rows=rows, ret_out=ret_out, att_o=att_o, att_p=att_p):
            for p in range(RET_PAIRS):
                mixed_ref[rows, pl.ds(MIX_RET + p * PAIR, PAIR)] = _retention_out(
                    p, rows, proj_ref, ret_out[p], ret_g).astype(BF16)
            for p in range(ATT_PAIRS):
                mixed_ref[rows, pl.ds(MIX_ATT + p * PAIR, PAIR)] = (
                    att_o[p] * att_p[p][1]).astype(BF16)
            for job in _out_proj_jobs(x_ref, mixed_ref, w_out_ref, o_ref, rows):
                job()

    finish_previous()
    _cast_blocks(cast_src, cast_dst, cast_kinds)


def _ffn_kernel(*refs, final_norm, cast_kinds):
    n_cast = len(cast_kinds)
    x_ref, g2_ref, w_in_ref, w_out_ref, gf_ref = refs[:5]
    cast_src = refs[5:5 + n_cast]
    o_ref = refs[5 + n_cast]
    cast_dst = refs[6 + n_cast:6 + 2 * n_cast]
    act_ref, h_ref = refs[6 + 2 * n_cast:]

    def gate_up(h, rows, j):
        a = _dot(h, w_in_ref[:, pl.ds(j * FF_BLOCK, FF_BLOCK)])
        b = _dot(h, w_in_ref[:, pl.ds(D_FF + j * FF_BLOCK, FF_BLOCK)])
        act_ref[rows, pl.ds(j * FF_BLOCK, FF_BLOCK)] = (jax.nn.silu(a) * b).astype(BF16)

    g2 = g2_ref[...]
    for r in range(FFN_TILE // FFN_NORM_ROWS):
        rows = pl.ds(r * FFN_NORM_ROWS, FFN_NORM_ROWS)
        h = _rms_norm(x_ref[rows, :], g2).astype(BF16)
        h_ref[rows, :] = h
        gate_up(h, rows, 0)
    for j in range(1, D_FF // FF_BLOCK):
        gate_up(h_ref[...], pl.ds(0, FFN_TILE), j)
    _cast_blocks(cast_src, cast_dst, cast_kinds)
    y = x_ref[...] + _dot(act_ref[...], w_out_ref[...])
    if final_norm:
        y = _rms_norm(y, gf_ref[...])
    o_ref[...] = y


def _layer_spec(stacked, layer):
    tail = (0,) * (stacked.ndim - 1)
    return pl.BlockSpec((None,) + stacked.shape[1:], lambda *_: (layer,) + tail,
                        pipeline_mode=pl.Buffered(1))


def _const_spec(arr):
    zeros = (0,) * arr.ndim
    return pl.BlockSpec(arr.shape, lambda *_: zeros, pipeline_mode=pl.Buffered(1))


CAST_KINDS = {"w_in": "pair_q_cols", "w_out": "pair_att_rows", "w_ffn_in": "plain",
              "w_ffn_out": "plain"}


def _paired_head_block(i, first):
    j = i - first
    return jnp.where(j < 0, i, first + (j % ATT_GROUP) * ATT_KV_HEADS + j // ATT_GROUP)


def _cast_specs(weights, kinds, layer, n_tiles):
    in_specs, out_specs, out_shapes = [], [], []
    for w, kind in zip(weights, kinds):
        rows, cols = w.shape[1] // n_tiles, w.shape[2]
        assert rows * n_tiles == w.shape[1] and rows % BF16_SUBLANES == 0
        in_specs.append(pl.BlockSpec((None, rows, cols), lambda i: (layer, i, 0)))
        if kind == "pair_att_rows":
            assert rows == HEAD_DIM and MIX_ATT % rows == 0
            out_specs.append(pl.BlockSpec(
                (rows, cols), lambda i: (_paired_head_block(i, MIX_ATT // HEAD_DIM), 0)))
        else:
            out_specs.append(pl.BlockSpec((rows, cols), lambda i: (i, 0)))
        out_shapes.append(jax.ShapeDtypeStruct(w.shape[1:], BF16))
    return in_specs, out_specs, out_shapes


def _first_cast_call(w_in, w_out):
    rows = w_in.shape[1] // FIRST_CAST_STEPS
    assert rows * FIRST_CAST_STEPS == w_in.shape[1] and rows % BF16_SUBLANES == 0
    return pl.pallas_call(
        _first_cast_kernel,
        grid=(FIRST_CAST_STEPS,),
        in_specs=[pl.BlockSpec((None, rows, w_in.shape[2]), lambda i: (0, i, 0)),
                  _layer_spec(w_out, 0)],
        out_specs=[pl.BlockSpec((rows, w_in.shape[2]), lambda i: (i, 0)),
                   pl.BlockSpec(w_out.shape[1:], lambda i: (0, 0))],
        out_shape=[jax.ShapeDtypeStruct(w_in.shape[1:], BF16),
                   jax.ShapeDtypeStruct(w_out.shape[1:], BF16)],
        compiler_params=pltpu.CompilerParams(dimension_semantics=("arbitrary",)),
        name="cast",
    )(w_in, w_out)


def _mixer_call(x, layer, seq, g1, w_in, w_out, ret_g, ln_g, ln_b, ws, bs, sinks, tables,
                cast_weights, cast_kinds, cast_layer):
    tokens = x.shape[0]
    n_tiles = tokens // MIX_TILE
    last_group = tokens // PROJ_ROWS - 1
    x_spec = pl.BlockSpec((MIX_TILE, D_MODEL), lambda s: (s, 0))
    next_spec = pl.BlockSpec(
        (PROJ_ROWS, D_MODEL),
        lambda s: (jnp.minimum((s + 1) * (MIX_TILE // PROJ_ROWS), last_group), 0))
    stacked = (ret_g, ln_g, ln_b, ws, bs)
    cast_in, cast_out, cast_shapes = _cast_specs(cast_weights, cast_kinds, cast_layer, n_tiles)
    out = pl.pallas_call(
        functools.partial(_mixer_kernel, layer=layer, tiles_per_seq=seq // MIX_TILE,
                          cast_kinds=cast_kinds),
        grid=(n_tiles,),
        in_specs=[x_spec, next_spec, _layer_spec(g1, layer), _const_spec(w_in), _const_spec(w_out)]
        + [_layer_spec(a, layer) for a in stacked]
        + [pl.BlockSpec(memory_space=pltpu.SMEM)] + [_const_spec(a) for a in tables] + cast_in,
        out_specs=[x_spec] + cast_out,
        out_shape=[jax.ShapeDtypeStruct(x.shape, x.dtype)] + cast_shapes,
        scratch_shapes=[
            pltpu.VMEM((MIX_TILE, IN_WIDTH), F32),
            pltpu.VMEM((MIX_TILE, D_MODEL), BF16),
            pltpu.VMEM((CHUNK, MLP_GROUPS * CHUNK), BF16),
            pltpu.VMEM((RET_PAIRS, PAIR, PAIR), F32),
            pltpu.VMEM((CHUNK, PAIR), BF16),
            pltpu.VMEM((CHUNK, PAIR), BF16),
        ],
        compiler_params=pltpu.CompilerParams(
            dimension_semantics=("arbitrary",),
            vmem_limit_bytes=MIX_VMEM_BYTES),
        name="mixer",
    )(x, x, g1, w_in, w_out, *stacked, sinks, *tables, *cast_weights)
    return out[0], out[1:]


def _ffn_call(x, layer, g2, w_in, w_out, gf, final_norm, cast_weights, cast_kinds):
    n_tiles = x.shape[0] // FFN_TILE
    x_spec = pl.BlockSpec((FFN_TILE, D_MODEL), lambda i: (i, 0))
    cast_in, cast_out, cast_shapes = _cast_specs(cast_weights, cast_kinds, layer + 1, n_tiles)
    out = pl.pallas_call(
        functools.partial(_ffn_kernel, final_norm=final_norm, cast_kinds=cast_kinds),
        grid=(n_tiles,),
        in_specs=[x_spec, _layer_spec(g2, layer), _const_spec(w_in), _const_spec(w_out),
                  _const_spec(gf)] + cast_in,
        out_specs=[x_spec] + cast_out,
        out_shape=[jax.ShapeDtypeStruct(x.shape, x.dtype)] + cast_shapes,
        scratch_shapes=[pltpu.VMEM((FFN_TILE, D_FF), BF16), pltpu.VMEM((FFN_TILE, D_MODEL), BF16)],
        compiler_params=pltpu.CompilerParams(
            dimension_semantics=("arbitrary",),
            vmem_limit_bytes=FFN_VMEM_BYTES),
        name="ffn",
    )(x, g2, w_in, w_out, gf, *cast_weights)
    return out[0], out[1:]


def _alibi_slopes(n):
    def pow2(m):
        start = 2.0 ** (-(2.0 ** -(math.log2(m) - 3)))
        return [start * start ** i for i in range(m)]
    if math.log2(n).is_integer():
        s = pow2(n)
    else:
        c = 2 ** int(math.floor(math.log2(n)))
        s = pow2(c) + pow2(2 * c)[0::2][: n - c]
    return np.array(s, dtype=np.float32)


def _tables():
    f32 = np.float32
    scale = f32(HEAD_DIM ** -0.5)
    log_g = np.log1p(-(f32(2.0) ** (-f32(RET_DECAY_BASE) - np.arange(RET_HEADS, dtype=f32))))
    pos = np.arange(CHUNK, dtype=f32)
    diff = pos[:, None] - pos[None, :]
    intra = np.where(diff[None] >= 0,
                     np.exp(log_g[:, None, None] * np.maximum(diff, f32(0.0))[None]), f32(0.0)) * scale
    intra_tab = intra.reshape(RET_PAIRS, 2, CHUNK, CHUNK).transpose(0, 2, 1, 3).reshape(
        RET_PAIRS, CHUNK, 2 * CHUNK)
    lane_gamma = np.repeat(log_g, HEAD_DIM).reshape(RET_PAIRS, 1, PAIR)
    head_tab = np.exp(lane_gamma * (pos + f32(1.0))[None, :, None])
    tail_tab = np.swapaxes(np.exp(lane_gamma * (f32(CHUNK - 1.0) - pos)[None, :, None]) * scale, 1, 2)
    sdec_tab = np.broadcast_to(
        np.exp(lane_gamma * f32(CHUNK)).reshape(RET_PAIRS, PAIR, 1), (RET_PAIRS, PAIR, PAIR))

    qi = np.arange(CHUNK)
    kj = np.arange(2 * CHUNK)
    dist = CHUNK + qi[:, None] - kj[None, :]
    in_window = (dist >= 0) & (dist < WINDOW)
    allowed = np.stack([in_window, in_window & (kj[None, :] >= CHUNK)])
    slopes = _alibi_slopes(ATT_HEADS)[np.asarray(ATT_HEAD_ORDER)]
    bias = np.where(allowed[:, None], -(slopes[None, :, None, None] * dist.astype(f32)),
                    f32(-np.inf))
    bias_tab = bias.reshape(2, ATT_PAIRS, 2, CHUNK, 2 * CHUNK).transpose(0, 1, 3, 2, 4).reshape(
        2, ATT_PAIRS, CHUNK, 4 * CHUNK)
    tabs = (intra_tab, head_tab, tail_tab, sdec_tab, bias_tab)
    assert all(t.dtype == f32 for t in tabs)
    return tuple(jnp.asarray(np.ascontiguousarray(t)) for t in tabs)


def kernel(x, norm1_g, w_in, ret_norm_g, mlp_ln_g, mlp_ln_b, w_spatial, b_spatial, attn_sinks,
           w_out, norm2_g, w_ffn_in, w_ffn_out, final_norm_g):
    batch, seq, d_model = x.shape
    depth = w_in.shape[0]
    tokens = batch * seq
    assert d_model == D_MODEL and seq % MIX_TILE == 0 and tokens % FFN_TILE == 0

    mix_f32, ffn_f32 = (w_in, w_out), (w_ffn_in, w_ffn_out)
    mix_kinds = (CAST_KINDS["w_in"], CAST_KINDS["w_out"])
    ffn_kinds = (CAST_KINDS["w_ffn_in"], CAST_KINDS["w_ffn_out"])
    mix_w = _first_cast_call(w_in, w_out)
    bs_tab = jnp.repeat(jnp.swapaxes(b_spatial, 1, 2), HEAD_DIM, axis=2)
    tables = _tables()
    rows = lambda a: a.reshape(depth, 1, -1)

    x = x.reshape(tokens, d_model)
    ffn_w = None
    for l in range(depth):
        first, last = l == 0, l == depth - 1
        x, cast = _mixer_call(x, l, seq, rows(norm1_g), mix_w[0], mix_w[1], rows(ret_norm_g),
                              rows(mlp_ln_g), rows(mlp_ln_b), w_spatial, bs_tab, attn_sinks, tables,
                              cast_weights=ffn_f32 if first else (),
                              cast_kinds=ffn_kinds if first else (), cast_layer=0)
        ffn_w = cast if first else ffn_w
        x, cast = _ffn_call(x, l, rows(norm2_g), ffn_w[0], ffn_w[1], final_norm_g.reshape(1, -1),
                            final_norm=last, cast_weights=() if last else mix_f32 + ffn_f32,
                            cast_kinds=() if last else mix_kinds + ffn_kinds)
        mix_w, ffn_w = cast[:2], cast[2:]
    return x.reshape(batch, seq, d_model)
```

```python
import functools
import math

import jax
import jax.numpy as jnp
import numpy as np
from jax import lax
from jax.experimental import pallas as pl
from jax.experimental.pallas import tpu as pltpu

D_MODEL = 1024
HEAD_DIM = 64
CHUNK = 128
RET_HEADS = 6
RET_WIDTH = RET_HEADS * HEAD_DIM
MLP_GROUPS = 4
MLP_WIDTH = MLP_GROUPS * HEAD_DIM
ATT_HEADS = 6
ATT_KV_HEADS = 2
ATT_WIDTH = ATT_HEADS * HEAD_DIM
ATT_KV_WIDTH = ATT_KV_HEADS * HEAD_DIM
WINDOW = 128
IN_WIDTH = 4 * RET_WIDTH + 2 * MLP_WIDTH + ATT_WIDTH + 2 * ATT_KV_WIDTH
D_FF = -(-8 * D_MODEL // (3 * 256)) * 256
RET_DECAY_BASE = 5.0
NORM_EPS = 1e-6

LANES = 128
BF16_SUBLANES = 16
PAIR = 2 * HEAD_DIM
assert PAIR == LANES and CHUNK == LANES and WINDOW == CHUNK

Q_R, K_R, V_R, G_R = 0, RET_WIDTH, 2 * RET_WIDTH, 3 * RET_WIDTH
Z_M = 4 * RET_WIDTH
Q_A = Z_M + 2 * MLP_WIDTH
K_A = Q_A + ATT_WIDTH
V_A = K_A + ATT_KV_WIDTH
MIX_RET, MIX_MLP, MIX_ATT = 0, RET_WIDTH, RET_WIDTH + MLP_WIDTH

RET_PAIRS = RET_HEADS // 2
MLP_PAIRS = MLP_GROUPS // 2
ATT_PAIRS = ATT_HEADS // 2
ATT_GROUP = ATT_HEADS // ATT_KV_HEADS
ATT_HEAD_ORDER = tuple(h for p in range(ATT_PAIRS) for h in (p, p + ATT_GROUP))

MIX_TILE = 1024
MIX_CHUNKS = MIX_TILE // CHUNK
PROJ_BLOCK = 512
PROJ_CHUNKS = 2
PROJ_ROWS = PROJ_CHUNKS * CHUNK
FILLS_PER_CHUNK = 3
FFN_TILE = 1024
FIRST_CAST_STEPS = 4
FFN_NORM_ROWS = 128
FF_BLOCK = 256
MIX_VMEM_BYTES = 52 * 1024 * 1024
FFN_VMEM_BYTES = 56 * 1024 * 1024

BF16 = jnp.bfloat16
F32 = jnp.float32


def _dot(a, b):
    return jnp.dot(a, b, preferred_element_type=F32)


def _left_half(shape):
    return lax.broadcasted_iota(jnp.int32, shape, len(shape) - 1) < HEAD_DIM


def _block_diag(pair):
    left = _left_half(pair.shape)
    zero = jnp.zeros_like(pair)
    return jnp.concatenate([jnp.where(left, pair, zero), jnp.where(left, zero, pair)], axis=0)


def _block_diag_t(pair_t):
    top = lax.broadcasted_iota(jnp.int32, pair_t.shape, 0) < HEAD_DIM
    zero = jnp.zeros_like(pair_t)
    return jnp.concatenate([jnp.where(top, pair_t, zero), jnp.where(top, zero, pair_t)], axis=1)


def _rms_norm(x, g):
    ms = jnp.mean(x * x, axis=-1, keepdims=True)
    return x * lax.rsqrt(ms + NORM_EPS) * g


def _gelu(x):
    return 0.5 * x * (1.0 + lax.erf(x * np.float32(math.sqrt(0.5))))


def _col_blocks(width, block):
    return [(c, min(block, width - c)) for c in range(0, width, block)]


def _in_proj_jobs(x, g1, w_in_ref, proj_ref, rows):
    h = _rms_norm(x, g1).astype(BF16)

    def piece(col, width):
        proj_ref[rows, pl.ds(col, width)] = _dot(h, w_in_ref[:, pl.ds(col, width)])

    return [functools.partial(piece, c, w) for c, w in _col_blocks(IN_WIDTH, PROJ_BLOCK)]


def _out_proj_jobs(x_ref, mixed_ref, w_out_ref, o_ref, rows):
    def piece(col, width):
        cols = pl.ds(col, width)
        o_ref[rows, cols] = x_ref[rows, cols] + _dot(mixed_ref[rows, :], w_out_ref[:, cols])

    return [functools.partial(piece, c, w) for c, w in _col_blocks(D_MODEL, PROJ_BLOCK)]


def _retention_scores(p, rows, proj_ref, state_ref, head_ref, tail_ref, sdec_ref):
    q = proj_ref[rows, pl.ds(Q_R + p * PAIR, PAIR)]
    k = proj_ref[rows, pl.ds(K_R + p * PAIR, PAIR)]
    vb = proj_ref[rows, pl.ds(V_R + p * PAIR, PAIR)].astype(BF16)
    kt = k.T
    scores = _dot(q.astype(BF16), _block_diag_t(kt.astype(BF16)))
    state = state_ref[p]
    cross = _dot((q * head_ref[p]).astype(BF16), state.astype(BF16))
    kv = _dot((kt * tail_ref[p]).astype(BF16), vb)
    same_head = _left_half((PAIR, PAIR)) == (
        lax.broadcasted_iota(jnp.int32, (PAIR, PAIR), 0) < HEAD_DIM)
    state_ref[p] = state * sdec_ref[p] + jnp.where(same_head, kv, 0.0)
    return scores, cross, vb


def _retention_out(p, rows, proj_ref, out, ret_g):
    sq = out * out
    lh = _left_half(out.shape)
    ss_l = jnp.sum(jnp.where(lh, sq, 0.0), axis=-1, keepdims=True)
    ss_r = jnp.sum(jnp.where(lh, 0.0, sq), axis=-1, keepdims=True)
    ms = jnp.where(lh, ss_l, ss_r) * np.float32(1.0 / HEAD_DIM)
    ret = out * lax.rsqrt(ms + NORM_EPS) * ret_g[:, p * PAIR:(p + 1) * PAIR]
    g = proj_ref[rows, pl.ds(G_R + p * PAIR, PAIR)]
    return jax.nn.silu(g) * ret


def _softmax_pair(p, s, sink_ref, layer):
    es, inv = [], []
    for j in range(2):
        sj = s[:, j * 2 * CHUNK:(j + 1) * 2 * CHUNK]
        sink = sink_ref[layer, ATT_HEAD_ORDER[2 * p + j]]
        m = jnp.maximum(jnp.max(sj, axis=-1, keepdims=True), sink)
        e = jnp.exp(sj - m)
        es.append(e.astype(BF16))
        inv.append(1.0 / (jnp.sum(e, axis=-1, keepdims=True) + jnp.exp(sink - m)))
    return jnp.concatenate(es, axis=1), jnp.where(_left_half((CHUNK, PAIR)), inv[0], inv[1])


def _cast_blocks(cast_src, cast_dst, kinds):
    for src, dst, kind in zip(cast_src, cast_dst, kinds):
        if kind == "pair_q_cols":
            dst[:, :Q_A] = src[:, :Q_A].astype(BF16)
            qa = src[:, Q_A:K_A]
            dst[:, Q_A:K_A] = jnp.concatenate(
                [qa[:, h * HEAD_DIM:(h + 1) * HEAD_DIM] for h in ATT_HEAD_ORDER], axis=1).astype(BF16)
            dst[:, K_A:] = src[:, K_A:].astype(BF16)
        else:
            dst[...] = src[...].astype(BF16)


def _first_cast_kernel(w_in_ref, w_out_ref, w_in_dst, w_out_dst):
    _cast_blocks((w_in_ref,), (w_in_dst,), (CAST_KINDS["w_in"],))

    @pl.when(pl.program_id(0) == 0)
    def _():
        w_out_dst[:MIX_ATT, :] = w_out_ref[:MIX_ATT, :].astype(BF16)
        for q, h in enumerate(ATT_HEAD_ORDER):
            w_out_dst[pl.ds(MIX_ATT + q * HEAD_DIM, HEAD_DIM), :] = (
                w_out_ref[pl.ds(MIX_ATT + h * HEAD_DIM, HEAD_DIM), :].astype(BF16))


def _mixer_kernel(*refs, layer, tiles_per_seq, cast_kinds):
    (x_ref, xn_ref, g1_ref, w_in_ref, w_out_ref, ret_g_ref, ln_g_ref, ln_b_ref, ws_ref,
     bs_ref, sink_ref, intra_ref, head_ref, tail_ref, sdec_ref, bias_ref) = refs[:16]
    n_cast = len(cast_kinds)
    cast_src = refs[16:16 + n_cast]
    o_ref = refs[16 + n_cast]
    cast_dst = refs[17 + n_cast:17 + 2 * n_cast]
    proj_ref, mixed_ref, wsb_ref, state_ref, kprev_ref, vprev_ref = refs[17 + 2 * n_cast:]
    step = pl.program_id(0)
    g1 = g1_ref[...]
    chunk_rows = [pl.ds(c * CHUNK, CHUNK) for c in range(MIX_CHUNKS)]
    group_rows = [pl.ds(g * PROJ_ROWS, PROJ_ROWS) for g in range(MIX_TILE // PROJ_ROWS)]

    @pl.when(step == 0)
    def _():
        causal = lax.broadcasted_iota(jnp.int32, (CHUNK, CHUNK), 0) >= lax.broadcasted_iota(
            jnp.int32, (CHUNK, CHUNK), 1)
        for g in range(MLP_GROUPS):
            wsb_ref[:, pl.ds(g * CHUNK, CHUNK)] = jnp.where(causal, ws_ref[g], 0.0).astype(BF16)
        for job in _in_proj_jobs(x_ref[group_rows[0], :], g1, w_in_ref, proj_ref, group_rows[0]):
            job()

    seq_start = step % tiles_per_seq == 0

    @pl.when(seq_start)
    def _():
        state_ref[...] = jnp.zeros_like(state_ref)
        kprev_ref[...] = jnp.zeros_like(kprev_ref)
        vprev_ref[...] = jnp.zeros_like(vprev_ref)

    ret_g = ret_g_ref[...]
    ln_g = ln_g_ref[...]
    ln_b = ln_b_ref[...]
    first_block = seq_start.astype(jnp.int32)

    def finish_previous():
        pass

    def start_in_proj(g):
        if g < len(group_rows):
            jobs = _in_proj_jobs(x_ref[group_rows[g], :], g1, w_in_ref, proj_ref, group_rows[g])
        else:
            jobs = _in_proj_jobs(xn_ref[...], g1, w_in_ref, proj_ref, group_rows[0])
        assert len(jobs) == PROJ_CHUNKS * FILLS_PER_CHUNK
        return iter(jobs)

    jobs = start_in_proj(1)
    for c in range(MIX_CHUNKS):
        rows = chunk_rows[c]

        def fill():
            next(jobs)()

        ret1 = [_retention_scores(p, rows, proj_ref, state_ref, head_ref, tail_ref, sdec_ref)
                for p in range(RET_PAIRS)]
        finish_previous()
        fill()

        kcur = (proj_ref[rows, pl.ds(K_A, PAIR)] * np.float32(HEAD_DIM ** -0.5)).T.astype(BF16)
        vcur = proj_ref[rows, pl.ds(V_A, PAIR)].astype(BF16)
        kbd = _block_diag_t(jnp.concatenate([kprev_ref[...], kcur], axis=1))
        vbd = _block_diag(jnp.concatenate([vprev_ref[...], vcur], axis=0))
        kprev_ref[...] = kcur
        vprev_ref[...] = vcur
        first = first_block if c == 0 else 0
        att_s = [_dot(proj_ref[rows, pl.ds(Q_A + p * PAIR, PAIR)].astype(BF16), kbd)
                 + bias_ref[first, p] for p in range(ATT_PAIRS)]
        z = _gelu(proj_ref[rows, pl.ds(Z_M, 2 * MLP_WIDTH)])
        u = z[:, :MLP_WIDTH]
        v = z[:, MLP_WIDTH:]
        mu = jnp.mean(v, axis=-1, keepdims=True)
        var = jnp.mean(jnp.square(v - mu), axis=-1, keepdims=True)
        vn = ((v - mu) * lax.rsqrt(var + NORM_EPS) * ln_g + ln_b).astype(BF16)

        ret_out = [_dot((scores * intra_ref[p]).astype(BF16), _block_diag(vb)) + cross
                   for p, (scores, cross, vb) in enumerate(ret1)]
        fill()

        for p in range(MLP_PAIRS):
            cols = slice(p * PAIR, (p + 1) * PAIR)
            gate = _dot(wsb_ref[:, pl.ds(2 * p * CHUNK, 2 * CHUNK)], _block_diag(vn[:, cols]))
            mixed_ref[rows, pl.ds(MIX_MLP + p * PAIR, PAIR)] = (
                u[:, cols] * (gate + bs_ref[:, cols])).astype(BF16)
        att_p = [_softmax_pair(p, att_s[p], sink_ref, layer) for p in range(ATT_PAIRS)]

        att_o = [_dot(e, vbd) for e, _ in att_p]
        fill()
        if c % PROJ_CHUNKS == PROJ_CHUNKS - 1 and c + 1 < MIX_CHUNKS:
            jobs = start_in_proj((c + 1) // PROJ_CHUNKS + 1)

        def finish_previous(rows=rows, ret_out=ret_out, att_o=att_o, att_p=att_p):
            for p in range(RET_PAIRS):
                mixed_ref[rows, pl.ds(MIX_RET + p * PAIR, PAIR)] = _retention_out(
                    p, rows, proj_ref, ret_out[p], ret_g).astype(BF16)
            for p in range(ATT_PAIRS):
                mixed_ref[rows, pl.ds(MIX_ATT + p * PAIR, PAIR)] = (
                    att_o[p] * att_p[p][1]).astype(BF16)
            for job in _out_proj_jobs(x_ref, mixed_ref, w_out_ref, o_ref, rows):
                job()

    finish_previous()
    _cast_blocks(cast_src, cast_dst, cast_kinds)


def _ffn_kernel(*refs, final_norm, cast_kinds):
    n_cast = len(cast_kinds)
    x_ref, g2_ref, w_in_ref, w_out_ref, gf_ref = refs[:5]
    cast_src = refs[5:5 + n_cast]
    o_ref = refs[5 + n_cast]
    cast_dst = refs[6 + n_cast:6 + 2 * n_cast]
    act_ref, h_ref = refs[6 + 2 * n_cast:]

    def gate_up(h, rows, j):
        a = _dot(h, w_in_ref[:, pl.ds(j * FF_BLOCK, FF_BLOCK)])
        b = _dot(h, w_in_ref[:, pl.ds(D_FF + j * FF_BLOCK, FF_BLOCK)])
        act_ref[rows, pl.ds(j * FF_BLOCK, FF_BLOCK)] = (jax.nn.silu(a) * b).astype(BF16)

    g2 = g2_ref[...]
    for r in range(FFN_TILE // FFN_NORM_ROWS):
        rows = pl.ds(r * FFN_NORM_ROWS, FFN_NORM_ROWS)
        h = _rms_norm(x_ref[rows, :], g2).astype(BF16)
        h_ref[rows, :] = h
        gate_up(h, rows, 0)
    for j in range(1, D_FF // FF_BLOCK):
        gate_up(h_ref[...], pl.ds(0, FFN_TILE), j)
    _cast_blocks(cast_src, cast_dst, cast_kinds)
    y = x_ref[...] + _dot(act_ref[...], w_out_ref[...])
    if final_norm:
        y = _rms_norm(y, gf_ref[...])
    o_ref[...] = y


def _layer_spec(stacked, layer):
    tail = (0,) * (stacked.ndim - 1)
    return pl.BlockSpec((None,) + stacked.shape[1:], lambda *_: (layer,) + tail,
                        pipeline_mode=pl.Buffered(1))


def _const_spec(arr):
    zeros = (0,) * arr.ndim
    return pl.BlockSpec(arr.shape, lambda *_: zeros, pipeline_mode=pl.Buffered(1))


CAST_KINDS = {"w_in": "pair_q_cols", "w_out": "pair_att_rows", "w_ffn_in": "plain",
              "w_ffn_out": "plain"}


def _paired_head_block(i, first):
    j = i - first
    return jnp.where(j < 0, i, first + (j % ATT_GROUP) * ATT_KV_HEADS + j // ATT_GROUP)


def _cast_specs(weights, kinds, layer, n_tiles):
    in_specs, out_specs, out_shapes = [], [], []
    for w, kind in zip(weights, kinds):
        rows, cols = w.shape[1] // n_tiles, w.shape[2]
        assert rows * n_tiles == w.shape[1] and rows % BF16_SUBLANES == 0
        in_specs.append(pl.BlockSpec((None, rows, cols), lambda i: (layer, i, 0)))
        if kind == "pair_att_rows":
            assert rows == HEAD_DIM and MIX_ATT % rows == 0
            out_specs.append(pl.BlockSpec(
                (rows, cols), lambda i: (_paired_head_block(i, MIX_ATT // HEAD_DIM), 0)))
        else:
            out_specs.append(pl.BlockSpec((rows, cols), lambda i: (i, 0)))
        out_shapes.append(jax.ShapeDtypeStruct(w.shape[1:], BF16))
    return in_specs, out_specs, out_shapes


def _first_cast_call(w_in, w_out):
    rows = w_in.shape[1] // FIRST_CAST_STEPS
    assert rows * FIRST_CAST_STEPS == w_in.shape[1] and rows % BF16_SUBLANES == 0
    return pl.pallas_call(
        _first_cast_kernel,
        grid=(FIRST_CAST_STEPS,),
        in_specs=[pl.BlockSpec((None, rows, w_in.shape[2]), lambda i: (0, i, 0)),
                  _layer_spec(w_out, 0)],
        out_specs=[pl.BlockSpec((rows, w_in.shape[2]), lambda i: (i, 0)),
                   pl.BlockSpec(w_out.shape[1:], lambda i: (0, 0))],
        out_shape=[jax.ShapeDtypeStruct(w_in.shape[1:], BF16),
                   jax.ShapeDtypeStruct(w_out.shape[1:], BF16)],
        compiler_params=pltpu.CompilerParams(dimension_semantics=("arbitrary",)),
        name="cast",
    )(w_in, w_out)


def _mixer_call(x, layer, seq, g1, w_in, w_out, ret_g, ln_g, ln_b, ws, bs, sinks, tables,
                cast_weights, cast_kinds, cast_layer):
    tokens = x.shape[0]
    n_tiles = tokens // MIX_TILE
    last_group = tokens // PROJ_ROWS - 1
    x_spec = pl.BlockSpec((MIX_TILE, D_MODEL), lambda s: (s, 0))
    next_spec = pl.BlockSpec(
        (PROJ_ROWS, D_MODEL),
        lambda s: (jnp.minimum((s + 1) * (MIX_TILE // PROJ_ROWS), last_group), 0))
    stacked = (ret_g, ln_g, ln_b, ws, bs)
    cast_in, cast_out, cast_shapes = _cast_specs(cast_weights, cast_kinds, cast_layer, n_tiles)
    out = pl.pallas_call(
        functools.partial(_mixer_kernel, layer=layer, tiles_per_seq=seq // MIX_TILE,
                          cast_kinds=cast_kinds),
        grid=(n_tiles,),
        in_specs=[x_spec, next_spec, _layer_spec(g1, layer), _const_spec(w_in), _const_spec(w_out)]
        + [_layer_spec(a, layer) for a in stacked]
        + [pl.BlockSpec(memory_space=pltpu.SMEM)] + [_const_spec(a) for a in tables] + cast_in,
        out_specs=[x_spec] + cast_out,
        out_shape=[jax.ShapeDtypeStruct(x.shape, x.dtype)] + cast_shapes,
        scratch_shapes=[
            pltpu.VMEM((MIX_TILE, IN_WIDTH), F32),
            pltpu.VMEM((MIX_TILE, D_MODEL), BF16),
            pltpu.VMEM((CHUNK, MLP_GROUPS * CHUNK), BF16),
            pltpu.VMEM((RET_PAIRS, PAIR, PAIR), F32),
            pltpu.VMEM((CHUNK, PAIR), BF16),
            pltpu.VMEM((CHUNK, PAIR), BF16),
        ],
        compiler_params=pltpu.CompilerParams(
            dimension_semantics=("arbitrary",),
            vmem_limit_bytes=MIX_VMEM_BYTES),
        name="mixer",
    )(x, x, g1, w_in, w_out, *stacked, sinks, *tables, *cast_weights)
    return out[0], out[1:]


def _ffn_call(x, layer, g2, w_in, w_out, gf, final_norm, cast_weights, cast_kinds):
    n_tiles = x.shape[0] // FFN_TILE
    x_spec = pl.BlockSpec((FFN_TILE, D_MODEL), lambda i: (i, 0))
    cast_in, cast_out, cast_shapes = _cast_specs(cast_weights, cast_kinds, layer + 1, n_tiles)
    out = pl.pallas_call(
        functools.partial(_ffn_kernel, final_norm=final_norm, cast_kinds=cast_kinds),
        grid=(n_tiles,),
        in_specs=[x_spec, _layer_spec(g2, layer), _const_spec(w_in), _const_spec(w_out),
                  _const_spec(gf)] + cast_in,
        out_specs=[x_spec] + cast_out,
        out_shape=[jax.ShapeDtypeStruct(x.shape, x.dtype)] + cast_shapes,
        scratch_shapes=[pltpu.VMEM((FFN_TILE, D_FF), BF16), pltpu.VMEM((FFN_TILE, D_MODEL), BF16)],
        compiler_params=pltpu.CompilerParams(
            dimension_semantics=("arbitrary",),
            vmem_limit_bytes=FFN_VMEM_BYTES),
        name="ffn",
    )(x, g2, w_in, w_out, gf, *cast_weights)
    return out[0], out[1:]


def _alibi_slopes(n):
    def pow2(m):
        start = 2.0 ** (-(2.0 ** -(math.log2(m) - 3)))
        return [start * start ** i for i in range(m)]
    if math.log2(n).is_integer():
        s = pow2(n)
    else:
        c = 2 ** int(math.floor(math.log2(n)))
        s = pow2(c) + pow2(2 * c)[0::2][: n - c]
    return np.array(s, dtype=np.float32)


def _tables():
    f32 = np.float32
    scale = f32(HEAD_DIM ** -0.5)
    log_g = np.log1p(-(f32(2.0) ** (-f32(RET_DECAY_BASE) - np.arange(RET_HEADS, dtype=f32))))
    pos = np.arange(CHUNK, dtype=f32)
    diff = pos[:, None] - pos[None, :]
    intra = np.where(diff[None] >= 0,
                     np.exp(log_g[:, None, None] * np.maximum(diff, f32(0.0))[None]), f32(0.0)) * scale
    intra_tab = intra.reshape(RET_PAIRS, 2, CHUNK, CHUNK).transpose(0, 2, 1, 3).reshape(
        RET_PAIRS, CHUNK, 2 * CHUNK)
    lane_gamma = np.repeat(log_g, HEAD_DIM).reshape(RET_PAIRS, 1, PAIR)
    head_tab = np.exp(lane_gamma * (pos + f32(1.0))[None, :, None])
    tail_tab = np.swapaxes(np.exp(lane_gamma * (f32(CHUNK - 1.0) - pos)[None, :, None]) * scale, 1, 2)
    sdec_tab = np.broadcast_to(
        np.exp(lane_gamma * f32(CHUNK)).reshape(RET_PAIRS, PAIR, 1), (RET_PAIRS, PAIR, PAIR))

    qi = np.arange(CHUNK)
    kj = np.arange(2 * CHUNK)
    dist = CHUNK + qi[:, None] - kj[None, :]
    in_window = (dist >= 0) & (dist < WINDOW)
    allowed = np.stack([in_window, in_window & (kj[None, :] >= CHUNK)])
    slopes = _alibi_slopes(ATT_HEADS)[np.asarray(ATT_HEAD_ORDER)]
    bias = np.where(allowed[:, None], -(slopes[None, :, None, None] * dist.astype(f32)),
                    f32(-np.inf))
    bias_tab = bias.reshape(2, ATT_PAIRS, 2, CHUNK, 2 * CHUNK).transpose(0, 1, 3, 2, 4).reshape(
        2, ATT_PAIRS, CHUNK, 4 * CHUNK)
    tabs = (intra_tab, head_tab, tail_tab, sdec_tab, bias_tab)
    assert all(t.dtype == f32 for t in tabs)
    return tuple(jnp.asarray(np.ascontiguousarray(t)) for t in tabs)


def kernel(x, norm1_g, w_in, ret_norm_g, mlp_ln_g, mlp_ln_b, w_spatial, b_spatial, attn_sinks,
           w_out, norm2_g, w_ffn_in, w_ffn_out, final_norm_g):
    batch, seq, d_model = x.shape
    depth = w_in.shape[0]
    tokens = batch * seq
    assert d_model == D_MODEL and seq % MIX_TILE == 0 and tokens % FFN_TILE == 0

    mix_f32, ffn_f32 = (w_in, w_out), (w_ffn_in, w_ffn_out)
    mix_kinds = (CAST_KINDS["w_in"], CAST_KINDS["w_out"])
    ffn_kinds = (CAST_KINDS["w_ffn_in"], CAST_KINDS["w_ffn_out"])
    mix_w = _first_cast_call(w_in, w_out)
    bs_tab = jnp.repeat(jnp.swapaxes(b_spatial, 1, 2), HEAD_DIM, axis=2)
    tables = _tables()
    rows = lambda a: a.reshape(depth, 1, -1)

    x = x.reshape(tokens, d_model)
    ffn_w = None
    for l in range(depth):
        first, last = l == 0, l == depth - 1
        x, cast = _mixer_call(x, l, seq, rows(norm1_g), mix_w[0], mix_w[1], rows(ret_norm_g),
                              rows(mlp_ln_g), rows(mlp_ln_b), w_spatial, bs_tab, attn_sinks, tables,
                              cast_weights=ffn_f32 if first else (),
                              cast_kinds=ffn_kinds if first else (), cast_layer=0)
        ffn_w = cast if first else ffn_w
        x, cast = _ffn_call(x, l, rows(norm2_g), ffn_w[0], ffn_w[1], final_norm_g.reshape(1, -1),
                            final_norm=last, cast_weights=() if last else mix_f32 + ffn_f32,
                            cast_kinds=() if last else mix_kinds + ffn_kinds)
        mix_w, ffn_w = cast[:2], cast[2:]
    return x.reshape(batch, seq, d_model)
```

```python
import functools
import math

import jax
import jax.numpy as jnp
import numpy as np
from jax import lax
from jax.experimental import pallas as pl
from jax.experimental.pallas import tpu as pltpu

D_MODEL = 1024
HEAD_DIM = 64
CHUNK = 128
RET_HEADS = 6
RET_WIDTH = RET_HEADS * HEAD_DIM
MLP_GROUPS = 4
MLP_WIDTH = MLP_GROUPS * HEAD_DIM
ATT_HEADS = 6
ATT_KV_HEADS = 2
ATT_WIDTH = ATT_HEADS * HEAD_DIM
ATT_KV_WIDTH = ATT_KV_HEADS * HEAD_DIM
WINDOW = 128
IN_WIDTH = 4 * RET_WIDTH + 2 * MLP_WIDTH + ATT_WIDTH + 2 * ATT_KV_WIDTH
D_FF = -(-8 * D_MODEL // (3 * 256)) * 256
RET_DECAY_BASE = 5.0
NORM_EPS = 1e-6

LANES = 128
BF16_SUBLANES = 16
PAIR = 2 * HEAD_DIM
assert PAIR == LANES and CHUNK == LANES and WINDOW == CHUNK

Q_R, K_R, V_R, G_R = 0, RET_WIDTH, 2 * RET_WIDTH, 3 * RET_WIDTH
Z_M = 4 * RET_WIDTH
Q_A = Z_M + 2 * MLP_WIDTH
K_A = Q_A + ATT_WIDTH
V_A = K_A + ATT_KV_WIDTH
MIX_RET, MIX_MLP, MIX_ATT = 0, RET_WIDTH, RET_WIDTH + MLP_WIDTH

RET_PAIRS = RET_HEADS // 2
MLP_PAIRS = MLP_GROUPS // 2
ATT_PAIRS = ATT_HEADS // 2
ATT_GROUP = ATT_HEADS // ATT_KV_HEADS
ATT_HEAD_ORDER = tuple(h for p in range(ATT_PAIRS) for h in (p, p + ATT_GROUP))

MIX_TILE = 1024
MIX_CHUNKS = MIX_TILE // CHUNK
PROJ_BLOCK = 512
PROJ_CHUNKS = 2
PROJ_ROWS = PROJ_CHUNKS * CHUNK
FILLS_PER_CHUNK = 3
FFN_TILE = 1024
FIRST_CAST_STEPS = 4
FFN_NORM_ROWS = 256
FF_BLOCK = 256
MIX_VMEM_BYTES = 52 * 1024 * 1024
FFN_VMEM_BYTES = 56 * 1024 * 1024

BF16 = jnp.bfloat16
F32 = jnp.float32


def _dot(a, b):
    return jnp.dot(a, b, preferred_element_type=F32)


def _left_half(shape):
    return lax.broadcasted_iota(jnp.int32, shape, len(shape) - 1) < HEAD_DIM


def _block_diag(pair):
    left = _left_half(pair.shape)
    zero = jnp.zeros_like(pair)
    return jnp.concatenate([jnp.where(left, pair, zero), jnp.where(left, zero, pair)], axis=0)


def _block_diag_t(pair_t):
    top = lax.broadcasted_iota(jnp.int32, pair_t.shape, 0) < HEAD_DIM
    zero = jnp.zeros_like(pair_t)
    return jnp.concatenate([jnp.where(top, pair_t, zero), jnp.where(top, zero, pair_t)], axis=1)


def _rms_norm(x, g):
    ms = jnp.mean(x * x, axis=-1, keepdims=True)
    return x * lax.rsqrt(ms + NORM_EPS) * g


def _gelu(x):
    return 0.5 * x * (1.0 + lax.erf(x * np.float32(math.sqrt(0.5))))


def _col_blocks(width, block):
    return [(c, min(block, width - c)) for c in range(0, width, block)]


def _in_proj_jobs(x, g1, w_in_ref, proj_ref, rows):
    h = _rms_norm(x, g1).astype(BF16)

    def piece(col, width):
        proj_ref[rows, pl.ds(col, width)] = _dot(h, w_in_ref[:, pl.ds(col, width)])

    return [functools.partial(piece, c, w) for c, w in _col_blocks(IN_WIDTH, PROJ_BLOCK)]


def _out_proj_jobs(x_ref, mixed_ref, w_out_ref, o_ref, rows):
    def piece(col, width):
        cols = pl.ds(col, width)
        o_ref[rows, cols] = x_ref[rows, cols] + _dot(mixed_ref[rows, :], w_out_ref[:, cols])

    return [functools.partial(piece, c, w) for c, w in _col_blocks(D_MODEL, PROJ_BLOCK)]


def _retention_scores(p, rows, proj_ref, state_ref, head_ref, tail_ref, sdec_ref):
    q = proj_ref[rows, pl.ds(Q_R + p * PAIR, PAIR)]
    k = proj_ref[rows, pl.ds(K_R + p * PAIR, PAIR)]
    vb = proj_ref[rows, pl.ds(V_R + p * PAIR, PAIR)].astype(BF16)
    kt = k.T
    scores = _dot(q.astype(BF16), _block_diag_t(kt.astype(BF16)))
    state = state_ref[p]
    cross = _dot((q * head_ref[p]).astype(BF16), state.astype(BF16))
    kv = _dot((kt * tail_ref[p]).astype(BF16), vb)
    same_head = _left_half((PAIR, PAIR)) == (
        lax.broadcasted_iota(jnp.int32, (PAIR, PAIR), 0) < HEAD_DIM)
    state_ref[p] = state * sdec_ref[p] + jnp.where(same_head, kv, 0.0)
    return scores, cross, vb


def _retention_out(p, rows, proj_ref, out, ret_g):
    sq = out * out
    lh = _left_half(out.shape)
    ss_l = jnp.sum(jnp.where(lh, sq, 0.0), axis=-1, keepdims=True)
    ss_r = jnp.sum(jnp.where(lh, 0.0, sq), axis=-1, keepdims=True)
    ms = jnp.where(lh, ss_l, ss_r) * np.float32(1.0 / HEAD_DIM)
    ret = out * lax.rsqrt(ms + NORM_EPS) * ret_g[:, p * PAIR:(p + 1) * PAIR]
    g = proj_ref[rows, pl.ds(G_R + p * PAIR, PAIR)]
    return jax.nn.silu(g) * ret


def _softmax_pair(p, s, sink_ref, layer):
    es, inv = [], []
    for j in range(2):
        sj = s[:, j * 2 * CHUNK:(j + 1) * 2 * CHUNK]
        sink = sink_ref[layer, ATT_HEAD_ORDER[2 * p + j]]
        m = jnp.maximum(jnp.max(sj, axis=-1, keepdims=True), sink)
        e = jnp.exp(sj - m)
        es.append(e.astype(BF16))
        inv.append(1.0 / (jnp.sum(e, axis=-1, keepdims=True) + jnp.exp(sink - m)))
    return jnp.concatenate(es, axis=1), jnp.where(_left_half((CHUNK, PAIR)), inv[0], inv[1])


def _cast_blocks(cast_src, cast_dst, kinds):
    for src, dst, kind in zip(cast_src, cast_dst, kinds):
        if kind == "pair_q_cols":
            dst[:, :Q_A] = src[:, :Q_A].astype(BF16)
            qa = src[:, Q_A:K_A]
            dst[:, Q_A:K_A] = jnp.concatenate(
                [qa[:, h * HEAD_DIM:(h + 1) * HEAD_DIM] for h in ATT_HEAD_ORDER], axis=1).astype(BF16)
            dst[:, K_A:] = src[:, K_A:].astype(BF16)
        else:
            dst[...] = src[...].astype(BF16)


def _first_cast_kernel(w_in_ref, w_out_ref, w_in_dst, w_out_dst):
    _cast_blocks((w_in_ref,), (w_in_dst,), (CAST_KINDS["w_in"],))

    @pl.when(pl.program_id(0) == 0)
    def _():
        w_out_dst[:MIX_ATT, :] = w_out_ref[:MIX_ATT, :].astype(BF16)
        for q, h in enumerate(ATT_HEAD_ORDER):
            w_out_dst[pl.ds(MIX_ATT + q * HEAD_DIM, HEAD_DIM), :] = (
                w_out_ref[pl.ds(MIX_ATT + h * HEAD_DIM, HEAD_DIM), :].astype(BF16))


def _mixer_kernel(*refs, layer, tiles_per_seq, cast_kinds):
    (x_ref, xn_ref, g1_ref, w_in_ref, w_out_ref, ret_g_ref, ln_g_ref, ln_b_ref, ws_ref,
     bs_ref, sink_ref, intra_ref, head_ref, tail_ref, sdec_ref, bias_ref) = refs[:16]
    n_cast = len(cast_kinds)
    cast_src = refs[16:16 + n_cast]
    o_ref = refs[16 + n_cast]
    cast_dst = refs[17 + n_cast:17 + 2 * n_cast]
    proj_ref, mixed_ref, wsb_ref, state_ref, kprev_ref, vprev_ref = refs[17 + 2 * n_cast:]
    step = pl.program_id(0)
    g1 = g1_ref[...]
    chunk_rows = [pl.ds(c * CHUNK, CHUNK) for c in range(MIX_CHUNKS)]
    group_rows = [pl.ds(g * PROJ_ROWS, PROJ_ROWS) for g in range(MIX_TILE // PROJ_ROWS)]

    @pl.when(step == 0)
    def _():
        causal = lax.broadcasted_iota(jnp.int32, (CHUNK, CHUNK), 0) >= lax.broadcasted_iota(
            jnp.int32, (CHUNK, CHUNK), 1)
        for g in range(MLP_GROUPS):
            wsb_ref[:, pl.ds(g * CHUNK, CHUNK)] = jnp.where(causal, ws_ref[g], 0.0).astype(BF16)
        for job in _in_proj_jobs(x_ref[group_rows[0], :], g1, w_in_ref, proj_ref, group_rows[0]):
            job()

    seq_start = step % tiles_per_seq == 0

    @pl.when(seq_start)
    def _():
        state_ref[...] = jnp.zeros_like(state_ref)
        kprev_ref[...] = jnp.zeros_like(kprev_ref)
        vprev_ref[...] = jnp.zeros_like(vprev_ref)

    ret_g = ret_g_ref[...]
    ln_g = ln_g_ref[...]
    ln_b = ln_b_ref[...]
    first_block = seq_start.astype(jnp.int32)

    def finish_previous():
        pass

    def start_in_proj(g):
        if g < len(group_rows):
            jobs = _in_proj_jobs(x_ref[group_rows[g], :], g1, w_in_ref, proj_ref, group_rows[g])
        else:
            jobs = _in_proj_jobs(xn_ref[...], g1, w_in_ref, proj_ref, group_rows[0])
        assert len(jobs) == PROJ_CHUNKS * FILLS_PER_CHUNK
        return iter(jobs)

    jobs = start_in_proj(1)
    for c in range(MIX_CHUNKS):
        rows = chunk_rows[c]

        def fill():
            next(jobs)()

        ret1 = [_retention_scores(p, rows, proj_ref, state_ref, head_ref, tail_ref, sdec_ref)
                for p in range(RET_PAIRS)]
        finish_previous()
        fill()

        kcur = (proj_ref[rows, pl.ds(K_A, PAIR)] * np.float32(HEAD_DIM ** -0.5)).T.astype(BF16)
        vcur = proj_ref[rows, pl.ds(V_A, PAIR)].astype(BF16)
        kbd = _block_diag_t(jnp.concatenate([kprev_ref[...], kcur], axis=1))
        vbd = _block_diag(jnp.concatenate([vprev_ref[...], vcur], axis=0))
        kprev_ref[...] = kcur
        vprev_ref[...] = vcur
        first = first_block if c == 0 else 0
        att_s = [_dot(proj_ref[rows, pl.ds(Q_A + p * PAIR, PAIR)].astype(BF16), kbd)
                 + bias_ref[first, p] for p in range(ATT_PAIRS)]
        z = _gelu(proj_ref[rows, pl.ds(Z_M, 2 * MLP_WIDTH)])
        u = z[:, :MLP_WIDTH]
        v = z[:, MLP_WIDTH:]
        mu = jnp.mean(v, axis=-1, keepdims=True)
        var = jnp.mean(jnp.square(v - mu), axis=-1, keepdims=True)
        vn = ((v - mu) * lax.rsqrt(var + NORM_EPS) * ln_g + ln_b).astype(BF16)

        ret_out = [_dot((scores * intra_ref[p]).astype(BF16), _block_diag(vb)) + cross
                   for p, (scores, cross, vb) in enumerate(ret1)]
        fill()

        for p in range(MLP_PAIRS):
            cols = slice(p * PAIR, (p + 1) * PAIR)
            gate = _dot(wsb_ref[:, pl.ds(2 * p * CHUNK, 2 * CHUNK)], _block_diag(vn[:, cols]))
            mixed_ref[rows, pl.ds(MIX_MLP + p * PAIR, PAIR)] = (
                u[:, cols] * (gate + bs_ref[:, cols])).astype(BF16)
        att_p = [_softmax_pair(p, att_s[p], sink_ref, layer) for p in range(ATT_PAIRS)]

        att_o = [_dot(e, vbd) for e, _ in att_p]
        fill()
        if c % PROJ_CHUNKS == PROJ_CHUNKS - 1 and c + 1 < MIX_CHUNKS:
            jobs = start_in_proj((c + 1) // PROJ_CHUNKS + 1)

        def finish_previous(rows=rows, ret_out=ret_out, att_o=att_o, att_p=att_p):
            for p in range(RET_PAIRS):
                mixed_ref[rows, pl.ds(MIX_RET + p * PAIR, PAIR)] = _retention_out(
                    p, rows, proj_ref, ret_out[p], ret_g).astype(BF16)
            for p in range(ATT_PAIRS):
                mixed_ref[rows, pl.ds(MIX_ATT + p * PAIR, PAIR)] = (
                    att_o[p] * att_p[p][1]).astype(BF16)
            for job in _out_proj_jobs(x_ref, mixed_ref, w_out_ref, o_ref, rows):
                job()

    finish_previous()
    _cast_blocks(cast_src, cast_dst, cast_kinds)


def _ffn_kernel(*refs, final_norm, cast_kinds):
    n_cast = len(cast_kinds)
    x_ref, g2_ref, w_in_ref, w_out_ref, gf_ref = refs[:5]
    cast_src = refs[5:5 + n_cast]
    o_ref = refs[5 + n_cast]
    cast_dst = refs[6 + n_cast:6 + 2 * n_cast]
    act_ref, h_ref = refs[6 + 2 * n_cast:]

    def gate_up(h, rows, j):
        a = _dot(h, w_in_ref[:, pl.ds(j * FF_BLOCK, FF_BLOCK)])
        b = _dot(h, w_in_ref[:, pl.ds(D_FF + j * FF_BLOCK, FF_BLOCK)])
        act_ref[rows, pl.ds(j * FF_BLOCK, FF_BLOCK)] = (jax.nn.silu(a) * b).astype(BF16)

    g2 = g2_ref[...]
    for r in range(FFN_TILE // FFN_NORM_ROWS):
        rows = pl.ds(r * FFN_NORM_ROWS, FFN_NORM_ROWS)
        h = _rms_norm(x_ref[rows, :], g2).astype(BF16)
        h_ref[rows, :] = h
        gate_up(h, rows, 0)
    for j in range(1, D_FF // FF_BLOCK):
        gate_up(h_ref[...], pl.ds(0, FFN_TILE), j)
    _cast_blocks(cast_src, cast_dst, cast_kinds)
    y = x_ref[...] + _dot(act_ref[...], w_out_ref[...])
    if final_norm:
        y = _rms_norm(y, gf_ref[...])
    o_ref[...] = y


def _layer_spec(stacked, layer):
    tail = (0,) * (stacked.ndim - 1)
    return pl.BlockSpec((None,) + stacked.shape[1:], lambda *_: (layer,) + tail,
                        pipeline_mode=pl.Buffered(1))


def _const_spec(arr):
    zeros = (0,) * arr.ndim
    return pl.BlockSpec(arr.shape, lambda *_: zeros, pipeline_mode=pl.Buffered(1))


CAST_KINDS = {"w_in": "pair_q_cols", "w_out": "pair_att_rows", "w_ffn_in": "plain",
              "w_ffn_out": "plain"}


def _paired_head_block(i, first):
    j = i - first
    return jnp.where(j < 0, i, first + (j % ATT_GROUP) * ATT_KV_HEADS + j // ATT_GROUP)


def _cast_specs(weights, kinds, layer, n_tiles):
    in_specs, out_specs, out_shapes = [], [], []
    for w, kind in zip(weights, kinds):
        rows, cols = w.shape[1] // n_tiles, w.shape[2]
        assert rows * n_tiles == w.shape[1] and rows % BF16_SUBLANES == 0
        in_specs.append(pl.BlockSpec((None, rows, cols), lambda i: (layer, i, 0)))
        if kind == "pair_att_rows":
            assert rows == HEAD_DIM and MIX_ATT % rows == 0
            out_specs.append(pl.BlockSpec(
                (rows, cols), lambda i: (_paired_head_block(i, MIX_ATT // HEAD_DIM), 0)))
        else:
            out_specs.append(pl.BlockSpec((rows, cols), lambda i: (i, 0)))
        out_shapes.append(jax.ShapeDtypeStruct(w.shape[1:], BF16))
    return in_specs, out_specs, out_shapes


def _first_cast_call(w_in, w_out):
    rows = w_in.shape[1] // FIRST_CAST_STEPS
    assert rows * FIRST_CAST_STEPS == w_in.shape[1] and rows % BF16_SUBLANES == 0
    return pl.pallas_call(
        _first_cast_kernel,
        grid=(FIRST_CAST_STEPS,),
        in_specs=[pl.BlockSpec((None, rows, w_in.shape[2]), lambda i: (0, i, 0)),
                  _layer_spec(w_out, 0)],
        out_specs=[pl.BlockSpec((rows, w_in.shape[2]), lambda i: (i, 0)),
                   pl.BlockSpec(w_out.shape[1:], lambda i: (0, 0))],
        out_shape=[jax.ShapeDtypeStruct(w_in.shape[1:], BF16),
                   jax.ShapeDtypeStruct(w_out.shape[1:], BF16)],
        compiler_params=pltpu.CompilerParams(dimension_semantics=("arbitrary",)),
        name="cast",
    )(w_in, w_out)


def _mixer_call(x, layer, seq, g1, w_in, w_out, ret_g, ln_g, ln_b, ws, bs, sinks, tables,
                cast_weights, cast_kinds, cast_layer):
    tokens = x.shape[0]
    n_tiles = tokens // MIX_TILE
    last_group = tokens // PROJ_ROWS - 1
    x_spec = pl.BlockSpec((MIX_TILE, D_MODEL), lambda s: (s, 0))
    next_spec = pl.BlockSpec(
        (PROJ_ROWS, D_MODEL),
        lambda s: (jnp.minimum((s + 1) * (MIX_TILE // PROJ_ROWS), last_group), 0))
    stacked = (ret_g, ln_g, ln_b, ws, bs)
    cast_in, cast_out, cast_shapes = _cast_specs(cast_weights, cast_kinds, cast_layer, n_tiles)
    out = pl.pallas_call(
        functools.partial(_mixer_kernel, layer=layer, tiles_per_seq=seq // MIX_TILE,
                          cast_kinds=cast_kinds),
        grid=(n_tiles,),
        in_specs=[x_spec, next_spec, _layer_spec(g1, layer), _const_spec(w_in), _const_spec(w_out)]
        + [_layer_spec(a, layer) for a in stacked]
        + [pl.BlockSpec(memory_space=pltpu.SMEM)] + [_const_spec(a) for a in tables] + cast_in,
        out_specs=[x_spec] + cast_out,
        out_shape=[jax.ShapeDtypeStruct(x.shape, x.dtype)] + cast_shapes,
        scratch_shapes=[
            pltpu.VMEM((MIX_TILE, IN_WIDTH), F32),
            pltpu.VMEM((MIX_TILE, D_MODEL), BF16),
            pltpu.VMEM((CHUNK, MLP_GROUPS * CHUNK), BF16),
            pltpu.VMEM((RET_PAIRS, PAIR, PAIR), F32),
            pltpu.VMEM((CHUNK, PAIR), BF16),
            pltpu.VMEM((CHUNK, PAIR), BF16),
        ],
        compiler_params=pltpu.CompilerParams(
            dimension_semantics=("arbitrary",),
            vmem_limit_bytes=MIX_VMEM_BYTES),
        name="mixer",
    )(x, x, g1, w_in, w_out, *stacked, sinks, *tables, *cast_weights)
    return out[0], out[1:]


def _ffn_call(x, layer, g2, w_in, w_out, gf, final_norm, cast_weights, cast_kinds):
    n_tiles = x.shape[0] // FFN_TILE
    x_spec = pl.BlockSpec((FFN_TILE, D_MODEL), lambda i: (i, 0))
    cast_in, cast_out, cast_shapes = _cast_specs(cast_weights, cast_kinds, layer + 1, n_tiles)
    out = pl.pallas_call(
        functools.partial(_ffn_kernel, final_norm=final_norm, cast_kinds=cast_kinds),
        grid=(n_tiles,),
        in_specs=[x_spec, _layer_spec(g2, layer), _const_spec(w_in), _const_spec(w_out),
                  _const_spec(gf)] + cast_in,
        out_specs=[x_spec] + cast_out,
        out_shape=[jax.ShapeDtypeStruct(x.shape, x.dtype)] + cast_shapes,
        scratch_shapes=[pltpu.VMEM((FFN_TILE, D_FF), BF16), pltpu.VMEM((FFN_TILE, D_MODEL), BF16)],
        compiler_params=pltpu.CompilerParams(
            dimension_semantics=("arbitrary",),
            vmem_limit_bytes=FFN_VMEM_BYTES),
        name="ffn",
    )(x, g2, w_in, w_out, gf, *cast_weights)
    return out[0], out[1:]


def _alibi_slopes(n):
    def pow2(m):
        start = 2.0 ** (-(2.0 ** -(math.log2(m) - 3)))
        return [start * start ** i for i in range(m)]
    if math.log2(n).is_integer():
        s = pow2(n)
    else:
        c = 2 ** int(math.floor(math.log2(n)))
        s = pow2(c) + pow2(2 * c)[0::2][: n - c]
    return np.array(s, dtype=np.float32)


def _tables():
    f32 = np.float32
    scale = f32(HEAD_DIM ** -0.5)
    log_g = np.log1p(-(f32(2.0) ** (-f32(RET_DECAY_BASE) - np.arange(RET_HEADS, dtype=f32))))
    pos = np.arange(CHUNK, dtype=f32)
    diff = pos[:, None] - pos[None, :]
    intra = np.where(diff[None] >= 0,
                     np.exp(log_g[:, None, None] * np.maximum(diff, f32(0.0))[None]), f32(0.0)) * scale
    intra_tab = intra.reshape(RET_PAIRS, 2, CHUNK, CHUNK).transpose(0, 2, 1, 3).reshape(
        RET_PAIRS, CHUNK, 2 * CHUNK)
    lane_gamma = np.repeat(log_g, HEAD_DIM).reshape(RET_PAIRS, 1, PAIR)
    head_tab = np.exp(lane_gamma * (pos + f32(1.0))[None, :, None])
    tail_tab = np.swapaxes(np.exp(lane_gamma * (f32(CHUNK - 1.0) - pos)[None, :, None]) * scale, 1, 2)
    sdec_tab = np.broadcast_to(
        np.exp(lane_gamma * f32(CHUNK)).reshape(RET_PAIRS, PAIR, 1), (RET_PAIRS, PAIR, PAIR))

    qi = np.arange(CHUNK)
    kj = np.arange(2 * CHUNK)
    dist = CHUNK + qi[:, None] - kj[None, :]
    in_window = (dist >= 0) & (dist < WINDOW)
    allowed = np.stack([in_window, in_window & (kj[None, :] >= CHUNK)])
    slopes = _alibi_slopes(ATT_HEADS)[np.asarray(ATT_HEAD_ORDER)]
    bias = np.where(allowed[:, None], -(slopes[None, :, None, None] * dist.astype(f32)),
                    f32(-np.inf))
    bias_tab = bias.reshape(2, ATT_PAIRS, 2, CHUNK, 2 * CHUNK).transpose(0, 1, 3, 2, 4).reshape(
        2, ATT_PAIRS, CHUNK, 4 * CHUNK)
    tabs = (intra_tab, head_tab, tail_tab, sdec_tab, bias_tab)
    assert all(t.dtype == f32 for t in tabs)
    return tuple(jnp.asarray(np.ascontiguousarray(t)) for t in tabs)


def kernel(x, norm1_g, w_in, ret_norm_g, mlp_ln_g, mlp_ln_b, w_spatial, b_spatial, attn_sinks,
           w_out, norm2_g, w_ffn_in, w_ffn_out, final_norm_g):
    batch, seq, d_model = x.shape
    depth = w_in.shape[0]
    tokens = batch * seq
    assert d_model == D_MODEL and seq % MIX_TILE == 0 and tokens % FFN_TILE == 0

    mix_f32, ffn_f32 = (w_in, w_out), (w_ffn_in, w_ffn_out)
    mix_kinds = (CAST_KINDS["w_in"], CAST_KINDS["w_out"])
    ffn_kinds = (CAST_KINDS["w_ffn_in"], CAST_KINDS["w_ffn_out"])
    mix_w = _first_cast_call(w_in, w_out)
    bs_tab = jnp.repeat(jnp.swapaxes(b_spatial, 1, 2), HEAD_DIM, axis=2)
    tables = _tables()
    rows = lambda a: a.reshape(depth, 1, -1)

    x = x.reshape(tokens, d_model)
    ffn_w = None
    for l in range(depth):
        first, last = l == 0, l == depth - 1
        x, cast = _mixer_call(x, l, seq, rows(norm1_g), mix_w[0], mix_w[1], rows(ret_norm_g),
                              rows(mlp_ln_g), rows(mlp_ln_b), w_spatial, bs_tab, attn_sinks, tables,
                              cast_weights=ffn_f32 if first else (),
                              cast_kinds=ffn_kinds if first else (), cast_layer=0)
        ffn_w = cast if first else ffn_w
        x, cast = _ffn_call(x, l, rows(norm2_g), ffn_w[0], ffn_w[1], final_norm_g.reshape(1, -1),
                            final_norm=last, cast_weights=() if last else mix_f32 + ffn_f32,
                            cast_kinds=() if last else mix_kinds + ffn_kinds)
        mix_w, ffn_w = cast[:2], cast[2:]
    return x.reshape(batch, seq, d_model)
```

```python
import functools
import math

import jax
import jax.numpy as jnp
import numpy as np
from jax import lax
from jax.experimental import pallas as pl
from jax.experimental.pallas import tpu as pltpu

D_MODEL = 1024
HEAD_DIM = 64
CHUNK = 128
RET_HEADS = 6
RET_WIDTH = RET_HEADS * HEAD_DIM
MLP_GROUPS = 4
MLP_WIDTH = MLP_GROUPS * HEAD_DIM
ATT_HEADS = 6
ATT_KV_HEADS = 2
ATT_WIDTH = ATT_HEADS * HEAD_DIM
ATT_KV_WIDTH = ATT_KV_HEADS * HEAD_DIM
WINDOW = 128
IN_WIDTH = 4 * RET_WIDTH + 2 * MLP_WIDTH + ATT_WIDTH + 2 * ATT_KV_WIDTH
D_FF = -(-8 * D_MODEL // (3 * 256)) * 256
RET_DECAY_BASE = 5.0
NORM_EPS = 1e-6

LANES = 128
BF16_SUBLANES = 16
PAIR = 2 * HEAD_DIM
assert PAIR == LANES and CHUNK == LANES and WINDOW == CHUNK

Q_R, K_R, V_R, G_R = 0, RET_WIDTH, 2 * RET_WIDTH, 3 * RET_WIDTH
Z_M = 4 * RET_WIDTH
Q_A = Z_M + 2 * MLP_WIDTH
K_A = Q_A + ATT_WIDTH
V_A = K_A + ATT_KV_WIDTH
MIX_RET, MIX_MLP, MIX_ATT = 0, RET_WIDTH, RET_WIDTH + MLP_WIDTH

RET_PAIRS = RET_HEADS // 2
MLP_PAIRS = MLP_GROUPS // 2
ATT_PAIRS = ATT_HEADS // 2
ATT_GROUP = ATT_HEADS // ATT_KV_HEADS
ATT_HEAD_ORDER = tuple(h for p in range(ATT_PAIRS) for h in (p, p + ATT_GROUP))

MIX_TILE = 1024
MIX_CHUNKS = MIX_TILE // CHUNK
PROJ_BLOCK = 512
PROJ_CHUNKS = 2
PROJ_ROWS = PROJ_CHUNKS * CHUNK
FILLS_PER_CHUNK = 3
FFN_TILE = 1024
FIRST_CAST_STEPS = 4
FFN_NORM_ROWS = 256
FF_BLOCK = 256
MIX_VMEM_BYTES = 52 * 1024 * 1024
FFN_VMEM_BYTES = 56 * 1024 * 1024

BF16 = jnp.bfloat16
F32 = jnp.float32


def _dot(a, b):
    return jnp.dot(a, b, preferred_element_type=F32)


def _left_half(shape):
    return lax.broadcasted_iota(jnp.int32, shape, len(shape) - 1) < HEAD_DIM


def _block_diag(pair):
    left = _left_half(pair.shape)
    zero = jnp.zeros_like(pair)
    return jnp.concatenate([jnp.where(left, pair, zero), jnp.where(left, zero, pair)], axis=0)


def _block_diag_t(pair_t):
    top = lax.broadcasted_iota(jnp.int32, pair_t.shape, 0) < HEAD_DIM
    zero = jnp.zeros_like(pair_t)
    return jnp.concatenate([jnp.where(top, pair_t, zero), jnp.where(top, zero, pair_t)], axis=1)


def _rms_norm(x, g):
    ms = jnp.mean(x * x, axis=-1, keepdims=True)
    return x * lax.rsqrt(ms + NORM_EPS) * g


def _gelu(x):
    return 0.5 * x * (1.0 + lax.erf(x * np.float32(math.sqrt(0.5))))


def _col_blocks(width, block):
    return [(c, min(block, width - c)) for c in range(0, width, block)]


def _in_proj_jobs(x, g1, w_in_ref, proj_ref, rows):
    h = _rms_norm(x, g1).astype(BF16)

    def piece(col, width):
        proj_ref[rows, pl.ds(col, width)] = _dot(h, w_in_ref[:, pl.ds(col, width)])

    return [functools.partial(piece, c, w) for c, w in _col_blocks(IN_WIDTH, PROJ_BLOCK)]


def _out_proj_jobs(x_ref, mixed_ref, w_out_ref, o_ref, rows):
    def piece(col, width):
        cols = pl.ds(col, width)
        o_ref[rows, cols] = x_ref[rows, cols] + _dot(mixed_ref[rows, :], w_out_ref[:, cols])

    return [functools.partial(piece, c, w) for c, w in _col_blocks(D_MODEL, PROJ_BLOCK)]


def _retention_scores(p, rows, proj_ref, state_ref, head_ref, tail_ref, sdec_ref):
    q = proj_ref[rows, pl.ds(Q_R + p * PAIR, PAIR)]
    k = proj_ref[rows, pl.ds(K_R + p * PAIR, PAIR)]
    vb = proj_ref[rows, pl.ds(V_R + p * PAIR, PAIR)].astype(BF16)
    kt = k.T
    scores = _dot(q.astype(BF16), _block_diag_t(kt.astype(BF16)))
    state = state_ref[p]
    cross = _dot((q * head_ref[p]).astype(BF16), state.astype(BF16))
    kv = _dot((kt * tail_ref[p]).astype(BF16), vb)
    same_head = _left_half((PAIR, PAIR)) == (
        lax.broadcasted_iota(jnp.int32, (PAIR, PAIR), 0) < HEAD_DIM)
    state_ref[p] = state * sdec_ref[p] + jnp.where(same_head, kv, 0.0)
    return scores, cross, vb


def _retention_out(p, rows, proj_ref, out, ret_g):
    sq = out * out
    lh = _left_half(out.shape)
    ss_l = jnp.sum(jnp.where(lh, sq, 0.0), axis=-1, keepdims=True)
    ss_r = jnp.sum(jnp.where(lh, 0.0, sq), axis=-1, keepdims=True)
    ms = jnp.where(lh, ss_l, ss_r) * np.float32(1.0 / HEAD_DIM)
    ret = out * lax.rsqrt(ms + NORM_EPS) * ret_g[:, p * PAIR:(p + 1) * PAIR]
    g = proj_ref[rows, pl.ds(G_R + p * PAIR, PAIR)]
    return jax.nn.silu(g) * ret


def _softmax_pair(p, s, sink_ref, layer):
    es, inv = [], []
    for j in range(2):
        sj = s[:, j * 2 * CHUNK:(j + 1) * 2 * CHUNK]
        sink = sink_ref[layer, ATT_HEAD_ORDER[2 * p + j]]
        m = jnp.maximum(jnp.max(sj, axis=-1, keepdims=True), sink)
        e = jnp.exp(sj - m)
        es.append(e.astype(BF16))
        inv.append(1.0 / (jnp.sum(e, axis=-1, keepdims=True) + jnp.exp(sink - m)))
    return jnp.concatenate(es, axis=1), jnp.where(_left_half((CHUNK, PAIR)), inv[0], inv[1])


def _cast_blocks(cast_src, cast_dst, kinds):
    for src, dst, kind in zip(cast_src, cast_dst, kinds):
        if kind == "pair_q_cols":
            dst[:, :Q_A] = src[:, :Q_A].astype(BF16)
            qa = src[:, Q_A:K_A]
            dst[:, Q_A:K_A] = jnp.concatenate(
                [qa[:, h * HEAD_DIM:(h + 1) * HEAD_DIM] for h in ATT_HEAD_ORDER], axis=1).astype(BF16)
            dst[:, K_A:] = src[:, K_A:].astype(BF16)
        else:
            dst[...] = src[...].astype(BF16)


def _first_cast_kernel(w_in_ref, w_out_ref, w_in_dst, w_out_dst):
    _cast_blocks((w_in_ref,), (w_in_dst,), (CAST_KINDS["w_in"],))

    @pl.when(pl.program_id(0) == 0)
    def _():
        w_out_dst[:MIX_ATT, :] = w_out_ref[:MIX_ATT, :].astype(BF16)
        for q, h in enumerate(ATT_HEAD_ORDER):
            w_out_dst[pl.ds(MIX_ATT + q * HEAD_DIM, HEAD_DIM), :] = (
                w_out_ref[pl.ds(MIX_ATT + h * HEAD_DIM, HEAD_DIM), :].astype(BF16))


def _mixer_kernel(*refs, layer, tiles_per_seq, cast_kinds):
    (x_ref, xn_ref, g1_ref, w_in_ref, w_out_ref, ret_g_ref, ln_g_ref, ln_b_ref, ws_ref,
     bs_ref, sink_ref, intra_ref, head_ref, tail_ref, sdec_ref, bias_ref) = refs[:16]
    n_cast = len(cast_kinds)
    cast_src = refs[16:16 + n_cast]
    o_ref = refs[16 + n_cast]
    cast_dst = refs[17 + n_cast:17 + 2 * n_cast]
    proj_ref, mixed_ref, wsb_ref, state_ref, kprev_ref, vprev_ref = refs[17 + 2 * n_cast:]
    step = pl.program_id(0)
    g1 = g1_ref[...]
    chunk_rows = [pl.ds(c * CHUNK, CHUNK) for c in range(MIX_CHUNKS)]
    group_rows = [pl.ds(g * PROJ_ROWS, PROJ_ROWS) for g in range(MIX_TILE // PROJ_ROWS)]

    @pl.when(step == 0)
    def _():
        causal = lax.broadcasted_iota(jnp.int32, (CHUNK, CHUNK), 0) >= lax.broadcasted_iota(
            jnp.int32, (CHUNK, CHUNK), 1)
        for g in range(MLP_GROUPS):
            wsb_ref[:, pl.ds(g * CHUNK, CHUNK)] = jnp.where(causal, ws_ref[g], 0.0).astype(BF16)
        for job in _in_proj_jobs(x_ref[group_rows[0], :], g1, w_in_ref, proj_ref, group_rows[0]):
            job()

    seq_start = step % tiles_per_seq == 0

    @pl.when(seq_start)
    def _():
        state_ref[...] = jnp.zeros_like(state_ref)
        kprev_ref[...] = jnp.zeros_like(kprev_ref)
        vprev_ref[...] = jnp.zeros_like(vprev_ref)

    ret_g = ret_g_ref[...]
    ln_g = ln_g_ref[...]
    ln_b = ln_b_ref[...]
    first_block = seq_start.astype(jnp.int32)

    def finish_previous():
        pass

    def start_in_proj(g):
        if g < len(group_rows):
            jobs = _in_proj_jobs(x_ref[group_rows[g], :], g1, w_in_ref, proj_ref, group_rows[g])
        else:
            jobs = _in_proj_jobs(xn_ref[...], g1, w_in_ref, proj_ref, group_rows[0])
        assert len(jobs) == PROJ_CHUNKS * FILLS_PER_CHUNK
        return iter(jobs)

    jobs = start_in_proj(1)
    for c in range(MIX_CHUNKS):
        rows = chunk_rows[c]

        def fill():
            next(jobs)()

        ret1 = [_retention_scores(p, rows, proj_ref, state_ref, head_ref, tail_ref, sdec_ref)
                for p in range(RET_PAIRS)]
        finish_previous()
        fill()

        kcur = (proj_ref[rows, pl.ds(K_A, PAIR)] * np.float32(HEAD_DIM ** -0.5)).T.astype(BF16)
        vcur = proj_ref[rows, pl.ds(V_A, PAIR)].astype(BF16)
        kbd = _block_diag_t(jnp.concatenate([kprev_ref[...], kcur], axis=1))
        vbd = _block_diag(jnp.concatenate([vprev_ref[...], vcur], axis=0))
        kprev_ref[...] = kcur
        vprev_ref[...] = vcur
        first = first_block if c == 0 else 0
        att_s = [_dot(proj_ref[rows, pl.ds(Q_A + p * PAIR, PAIR)].astype(BF16), kbd)
                 + bias_ref[first, p] for p in range(ATT_PAIRS)]
        z = _gelu(proj_ref[rows, pl.ds(Z_M, 2 * MLP_WIDTH)])
        u = z[:, :MLP_WIDTH]
        v = z[:, MLP_WIDTH:]
        mu = jnp.mean(v, axis=-1, keepdims=True)
        var = jnp.mean(jnp.square(v - mu), axis=-1, keepdims=True)
        vn = ((v - mu) * lax.rsqrt(var + NORM_EPS) * ln_g + ln_b).astype(BF16)

        ret_out = [_dot((scores * intra_ref[p]).astype(BF16), _block_diag(vb)) + cross
                   for p, (scores, cross, vb) in enumerate(ret1)]
        fill()

        for p in range(MLP_PAIRS):
            cols = slice(p * PAIR, (p + 1) * PAIR)
            gate = _dot(wsb_ref[:, pl.ds(2 * p * CHUNK, 2 * CHUNK)], _block_diag(vn[:, cols]))
            mixed_ref[rows, pl.ds(MIX_MLP + p * PAIR, PAIR)] = (
                u[:, cols] * (gate + bs_ref[:, cols])).astype(BF16)
        att_p = [_softmax_pair(p, att_s[p], sink_ref, layer) for p in range(ATT_PAIRS)]

        att_o = [_dot(e, vbd) for e, _ in att_p]
        fill()
        if c % PROJ_CHUNKS == PROJ_CHUNKS - 1 and c + 1 < MIX_CHUNKS:
            jobs = start_in_proj((c + 1) // PROJ_CHUNKS + 1)

        def finish_previous(c=c, rows=rows, ret_out=ret_out, att_o=att_o, att_p=att_p):
            for p in range(RET_PAIRS):
                mixed_ref[rows, pl.ds(MIX_RET + p * PAIR, PAIR)] = _retention_out(
                    p, rows, proj_ref, ret_out[p], ret_g).astype(BF16)
            for p in range(ATT_PAIRS):
                mixed_ref[rows, pl.ds(MIX_ATT + p * PAIR, PAIR)] = (
                    att_o[p] * att_p[p][1]).astype(BF16)
            if c % PROJ_CHUNKS == PROJ_CHUNKS - 1:
                for job in _out_proj_jobs(x_ref, mixed_ref, w_out_ref, o_ref,
                                          group_rows[c // PROJ_CHUNKS]):
                    job()

    finish_previous()
    _cast_blocks(cast_src, cast_dst, cast_kinds)


def _ffn_kernel(*refs, final_norm, cast_kinds):
    n_cast = len(cast_kinds)
    x_ref, g2_ref, w_in_ref, w_out_ref, gf_ref = refs[:5]
    cast_src = refs[5:5 + n_cast]
    o_ref = refs[5 + n_cast]
    cast_dst = refs[6 + n_cast:6 + 2 * n_cast]
    act_ref, h_ref = refs[6 + 2 * n_cast:]

    def gate_up(h, rows, j):
        a = _dot(h, w_in_ref[:, pl.ds(j * FF_BLOCK, FF_BLOCK)])
        b = _dot(h, w_in_ref[:, pl.ds(D_FF + j * FF_BLOCK, FF_BLOCK)])
        act_ref[rows, pl.ds(j * FF_BLOCK, FF_BLOCK)] = (jax.nn.silu(a) * b).astype(BF16)

    g2 = g2_ref[...]
    for r in range(FFN_TILE // FFN_NORM_ROWS):
        rows = pl.ds(r * FFN_NORM_ROWS, FFN_NORM_ROWS)
        h = _rms_norm(x_ref[rows, :], g2).astype(BF16)
        h_ref[rows, :] = h
        gate_up(h, rows, 0)
    for j in range(1, D_FF // FF_BLOCK):
        gate_up(h_ref[...], pl.ds(0, FFN_TILE), j)
    _cast_blocks(cast_src, cast_dst, cast_kinds)
    y = x_ref[...] + _dot(act_ref[...], w_out_ref[...])
    if final_norm:
        y = _rms_norm(y, gf_ref[...])
    o_ref[...] = y


def _layer_spec(stacked, layer):
    tail = (0,) * (stacked.ndim - 1)
    return pl.BlockSpec((None,) + stacked.shape[1:], lambda *_: (layer,) + tail,
                        pipeline_mode=pl.Buffered(1))


def _const_spec(arr):
    zeros = (0,) * arr.ndim
    return pl.BlockSpec(arr.shape, lambda *_: zeros, pipeline_mode=pl.Buffered(1))


CAST_KINDS = {"w_in": "pair_q_cols", "w_out": "pair_att_rows", "w_ffn_in": "plain",
              "w_ffn_out": "plain"}


def _paired_head_block(i, first):
    j = i - first
    return jnp.where(j < 0, i, first + (j % ATT_GROUP) * ATT_KV_HEADS + j // ATT_GROUP)


def _cast_specs(weights, kinds, layer, n_tiles):
    in_specs, out_specs, out_shapes = [], [], []
    for w, kind in zip(weights, kinds):
        rows, cols = w.shape[1] // n_tiles, w.shape[2]
        assert rows * n_tiles == w.shape[1] and rows % BF16_SUBLANES == 0
        in_specs.append(pl.BlockSpec((None, rows, cols), lambda i: (layer, i, 0)))
        if kind == "pair_att_rows":
            assert rows == HEAD_DIM and MIX_ATT % rows == 0
            out_specs.append(pl.BlockSpec(
                (rows, cols), lambda i: (_paired_head_block(i, MIX_ATT // HEAD_DIM), 0)))
        else:
            out_specs.append(pl.BlockSpec((rows, cols), lambda i: (i, 0)))
        out_shapes.append(jax.ShapeDtypeStruct(w.shape[1:], BF16))
    return in_specs, out_specs, out_shapes


def _first_cast_call(w_in, w_out):
    rows = w_in.shape[1] // FIRST_CAST_STEPS
    assert rows * FIRST_CAST_STEPS == w_in.shape[1] and rows % BF16_SUBLANES == 0
    return pl.pallas_call(
        _first_cast_kernel,
        grid=(FIRST_CAST_STEPS,),
        in_specs=[pl.BlockSpec((None, rows, w_in.shape[2]), lambda i: (0, i, 0)),
                  _layer_spec(w_out, 0)],
        out_specs=[pl.BlockSpec((rows, w_in.shape[2]), lambda i: (i, 0)),
                   pl.BlockSpec(w_out.shape[1:], lambda i: (0, 0))],
        out_shape=[jax.ShapeDtypeStruct(w_in.shape[1:], BF16),
                   jax.ShapeDtypeStruct(w_out.shape[1:], BF16)],
        compiler_params=pltpu.CompilerParams(dimension_semantics=("arbitrary",)),
        name="cast",
    )(w_in, w_out)


def _mixer_call(x, layer, seq, g1, w_in, w_out, ret_g, ln_g, ln_b, ws, bs, sinks, tables,
                cast_weights, cast_kinds, cast_layer):
    tokens = x.shape[0]
    n_tiles = tokens // MIX_TILE
    last_group = tokens // PROJ_ROWS - 1
    x_spec = pl.BlockSpec((MIX_TILE, D_MODEL), lambda s: (s, 0))
    next_spec = pl.BlockSpec(
        (PROJ_ROWS, D_MODEL),
        lambda s: (jnp.minimum((s + 1) * (MIX_TILE // PROJ_ROWS), last_group), 0))
    stacked = (ret_g, ln_g, ln_b, ws, bs)
    cast_in, cast_out, cast_shapes = _cast_specs(cast_weights, cast_kinds, cast_layer, n_tiles)
    out = pl.pallas_call(
        functools.partial(_mixer_kernel, layer=layer, tiles_per_seq=seq // MIX_TILE,
                          cast_kinds=cast_kinds),
        grid=(n_tiles,),
        in_specs=[x_spec, next_spec, _layer_spec(g1, layer), _const_spec(w_in), _const_spec(w_out)]
        + [_layer_spec(a, layer) for a in stacked]
        + [pl.BlockSpec(memory_space=pltpu.SMEM)] + [_const_spec(a) for a in tables] + cast_in,
        out_specs=[x_spec] + cast_out,
        out_shape=[jax.ShapeDtypeStruct(x.shape, x.dtype)] + cast_shapes,
        scratch_shapes=[
            pltpu.VMEM((MIX_TILE, IN_WIDTH), F32),
            pltpu.VMEM((MIX_TILE, D_MODEL), BF16),
            pltpu.VMEM((CHUNK, MLP_GROUPS * CHUNK), BF16),
            pltpu.VMEM((RET_PAIRS, PAIR, PAIR), F32),
            pltpu.VMEM((CHUNK, PAIR), BF16),
            pltpu.VMEM((CHUNK, PAIR), BF16),
        ],
        compiler_params=pltpu.CompilerParams(
            dimension_semantics=("arbitrary",),
            vmem_limit_bytes=MIX_VMEM_BYTES),
        name="mixer",
    )(x, x, g1, w_in, w_out, *stacked, sinks, *tables, *cast_weights)
    return out[0], out[1:]


def _ffn_call(x, layer, g2, w_in, w_out, gf, final_norm, cast_weights, cast_kinds):
    n_tiles = x.shape[0] // FFN_TILE
    x_spec = pl.BlockSpec((FFN_TILE, D_MODEL), lambda i: (i, 0))
    cast_in, cast_out, cast_shapes = _cast_specs(cast_weights, cast_kinds, layer + 1, n_tiles)
    out = pl.pallas_call(
        functools.partial(_ffn_kernel, final_norm=final_norm, cast_kinds=cast_kinds),
        grid=(n_tiles,),
        in_specs=[x_spec, _layer_spec(g2, layer), _const_spec(w_in), _const_spec(w_out),
                  _const_spec(gf)] + cast_in,
        out_specs=[x_spec] + cast_out,
        out_shape=[jax.ShapeDtypeStruct(x.shape, x.dtype)] + cast_shapes,
        scratch_shapes=[pltpu.VMEM((FFN_TILE, D_FF), BF16), pltpu.VMEM((FFN_TILE, D_MODEL), BF16)],
        compiler_params=pltpu.CompilerParams(
            dimension_semantics=("arbitrary",),
            vmem_limit_bytes=FFN_VMEM_BYTES),
        name="ffn",
    )(x, g2, w_in, w_out, gf, *cast_weights)
    return out[0], out[1:]


def _alibi_slopes(n):
    def pow2(m):
        start = 2.0 ** (-(2.0 ** -(math.log2(m) - 3)))
        return [start * start ** i for i in range(m)]
    if math.log2(n).is_integer():
        s = pow2(n)
    else:
        c = 2 ** int(math.floor(math.log2(n)))
        s = pow2(c) + pow2(2 * c)[0::2][: n - c]
    return np.array(s, dtype=np.float32)


def _tables():
    f32 = np.float32
    scale = f32(HEAD_DIM ** -0.5)
    log_g = np.log1p(-(f32(2.0) ** (-f32(RET_DECAY_BASE) - np.arange(RET_HEADS, dtype=f32))))
    pos = np.arange(CHUNK, dtype=f32)
    diff = pos[:, None] - pos[None, :]
    intra = np.where(diff[None] >= 0,
                     np.exp(log_g[:, None, None] * np.maximum(diff, f32(0.0))[None]), f32(0.0)) * scale
    intra_tab = intra.reshape(RET_PAIRS, 2, CHUNK, CHUNK).transpose(0, 2, 1, 3).reshape(
        RET_PAIRS, CHUNK, 2 * CHUNK)
    lane_gamma = np.repeat(log_g, HEAD_DIM).reshape(RET_PAIRS, 1, PAIR)
    head_tab = np.exp(lane_gamma * (pos + f32(1.0))[None, :, None])
    tail_tab = np.swapaxes(np.exp(lane_gamma * (f32(CHUNK - 1.0) - pos)[None, :, None]) * scale, 1, 2)
    sdec_tab = np.broadcast_to(
        np.exp(lane_gamma * f32(CHUNK)).reshape(RET_PAIRS, PAIR, 1), (RET_PAIRS, PAIR, PAIR))

    qi = np.arange(CHUNK)
    kj = np.arange(2 * CHUNK)
    dist = CHUNK + qi[:, None] - kj[None, :]
    in_window = (dist >= 0) & (dist < WINDOW)
    allowed = np.stack([in_window, in_window & (kj[None, :] >= CHUNK)])
    slopes = _alibi_slopes(ATT_HEADS)[np.asarray(ATT_HEAD_ORDER)]
    bias = np.where(allowed[:, None], -(slopes[None, :, None, None] * dist.astype(f32)),
                    f32(-np.inf))
    bias_tab = bias.reshape(2, ATT_PAIRS, 2, CHUNK, 2 * CHUNK).transpose(0, 1, 3, 2, 4).reshape(
        2, ATT_PAIRS, CHUNK, 4 * CHUNK)
    tabs = (intra_tab, head_tab, tail_tab, sdec_tab, bias_tab)
    assert all(t.dtype == f32 for t in tabs)
    return tuple(jnp.asarray(np.ascontiguousarray(t)) for t in tabs)


def kernel(x, norm1_g, w_in, ret_norm_g, mlp_ln_g, mlp_ln_b, w_spatial, b_spatial, attn_sinks,
           w_out, norm2_g, w_ffn_in, w_ffn_out, final_norm_g):
    batch, seq, d_model = x.shape
    depth = w_in.shape[0]
    tokens = batch * seq
    assert d_model == D_MODEL and seq % MIX_TILE == 0 and tokens % FFN_TILE == 0

    mix_f32, ffn_f32 = (w_in, w_out), (w_ffn_in, w_ffn_out)
    mix_kinds = (CAST_KINDS["w_in"], CAST_KINDS["w_out"])
    ffn_kinds = (CAST_KINDS["w_ffn_in"], CAST_KINDS["w_ffn_out"])
    mix_w = _first_cast_call(w_in, w_out)
    bs_tab = jnp.repeat(jnp.swapaxes(b_spatial, 1, 2), HEAD_DIM, axis=2)
    tables = _tables()
    rows = lambda a: a.reshape(depth, 1, -1)

    x = x.reshape(tokens, d_model)
    ffn_w = None
    for l in range(depth):
        first, last = l == 0, l == depth - 1
        x, cast = _mixer_call(x, l, seq, rows(norm1_g), mix_w[0], mix_w[1], rows(ret_norm_g),
                              rows(mlp_ln_g), rows(mlp_ln_b), w_spatial, bs_tab, attn_sinks, tables,
                              cast_weights=ffn_f32 if first else (),
                              cast_kinds=ffn_kinds if first else (), cast_layer=0)
        ffn_w = cast if first else ffn_w
        x, cast = _ffn_call(x, l, rows(norm2_g), ffn_w[0], ffn_w[1], final_norm_g.reshape(1, -1),
                            final_norm=last, cast_weights=() if last else mix_f32 + ffn_f32,
                            cast_kinds=() if last else mix_kinds + ffn_kinds)
        mix_w, ffn_w = cast[:2], cast[2:]
    return x.reshape(batch, seq, d_model)
```
